```python
import jax, jax.numpy as jnp
from jax import lax
import numpy as np

D_MODEL = 1024
BATCH = 1
SEQ = 16384
DEPTH = 4

N_MIXERS = 3
D_FF = 2816
EPS = 1e-6
CONV_WIDTH = 31
GLA_HEADS = 4
GLA_DK = D_MODEL // 2
GLA_DV = D_MODEL
GLA_HK = GLA_DK // GLA_HEADS
GLA_HV = GLA_DV // GLA_HEADS
GLA_GATE_RANK = 16
GLA_TAU = 16.0
GLA_CHUNK = 64
MOBA_HEADS = 8
MOBA_HEAD_DIM = D_MODEL // MOBA_HEADS
MOBA_BLOCK = 256
MOBA_TOPK = 3
MOBA_QCHUNK = 32

N_CONV = len(range(0, DEPTH, N_MIXERS))
N_GLA = len(range(1, DEPTH, N_MIXERS))
N_MOBA = len(range(2, DEPTH, N_MIXERS))

kernel_name = "hybrid_conv_gla_moba_macaron"


def rmsnorm(x, g):
    xf = x.astype(jnp.float32)
    y = xf * lax.rsqrt(jnp.mean(xf * xf, axis=-1, keepdims=True) + EPS)
    return (y * g.astype(jnp.float32)).astype(x.dtype)


def layernorm(x, g, b):
    xf = x.astype(jnp.float32)
    mu = jnp.mean(xf, axis=-1, keepdims=True)
    var = jnp.mean(jnp.square(xf - mu), axis=-1, keepdims=True)
    y = (xf - mu) * lax.rsqrt(var + EPS) * g.astype(jnp.float32) + b.astype(jnp.float32)
    return y.astype(x.dtype)


def swiglu(x, w_gate, w_up, w_down):
    return (jax.nn.silu(x @ w_gate) * (x @ w_up)) @ w_down


def conformer_conv(x, w_pw1, b_pw1, w_dw, b_dw, ln_g, ln_b, w_pw2, b_pw2):
    a, gt = jnp.split(x @ w_pw1 + b_pw1, 2, axis=-1)
    h = a * jax.nn.sigmoid(gt)
    h = lax.conv_general_dilated(
        h, w_dw[:, None, :].astype(h.dtype), window_strides=(1,),
        padding=[(CONV_WIDTH - 1, 0)],
        dimension_numbers=("NWC", "WIO", "NWC"),
        feature_group_count=D_MODEL) + b_dw
    h = jax.nn.silu(layernorm(h, ln_g, ln_b))
    return h @ w_pw2 + b_pw2


def gla_mixer(x, w_q, w_k, w_v, w_a1, w_a2, b_a, w_r, b_r, norm_g, w_o):
    B, S, _ = x.shape
    nc = S // GLA_CHUNK
    f32 = jnp.float32

    def heads(t, hd):
        return t.reshape(B, nc, GLA_CHUNK, GLA_HEADS, hd).transpose(1, 0, 3, 2, 4)

    q = heads((x @ w_q).astype(f32) * GLA_HK ** -0.5, GLA_HK)
    k = heads((x @ w_k).astype(f32), GLA_HK)
    v = heads((x @ w_v).astype(f32), GLA_HV)
    log_a = heads(jax.nn.log_sigmoid(((x @ w_a1) @ w_a2 + b_a).astype(f32)) / GLA_TAU, GLA_HK)
    causal = jnp.tril(jnp.ones((GLA_CHUNK, GLA_CHUNK), dtype=bool))

    def step(state, inp):
        qc, kc, vc, lac = inp
        b = jnp.cumsum(lac, axis=2)
        diff = b[:, :, :, None, :] - b[:, :, None, :, :]
        decay = jnp.exp(jnp.where(causal[:, :, None], diff, -jnp.inf))
        att = jnp.einsum("bhtd,bhsd,bhtsd->bhts", qc, kc, decay)
        o = att @ vc + jnp.einsum("bhtd,bhdv->bhtv", qc * jnp.exp(b), state)
        b_last = b[:, :, -1:, :]
        state = (jnp.exp(b_last[:, :, 0, :, None]) * state
                 + jnp.einsum("bhsd,bhsv->bhdv", kc * jnp.exp(b_last - b), vc))
        return state, o

    s0 = jnp.zeros((B, GLA_HEADS, GLA_HK, GLA_HV), f32)
    _, o = lax.scan(step, s0, (q, k, v, log_a))
    o = o.transpose(1, 0, 3, 2, 4).reshape(B, S, GLA_HEADS, GLA_HV)
    o = rmsnorm(o, norm_g).reshape(B, S, GLA_DV).astype(x.dtype)
    r = jax.nn.silu(x @ w_r + b_r)
    return (o * r) @ w_o


def moba_mixer(x, w_q, w_k, w_v, w_o):
    B, S, _ = x.shape
    H, Dh, L, Cq = MOBA_HEADS, MOBA_HEAD_DIM, MOBA_BLOCK, MOBA_QCHUNK
    f32 = jnp.float32
    nb = -(-S // L)
    sp = nb * L
    nq = S // Cq
    slopes = 2.0 ** (-8.0 * jnp.arange(1, H + 1, dtype=f32) / H)

    def heads(t):
        return t.reshape(B, S, H, Dh).transpose(0, 2, 1, 3).astype(f32)

    q = heads(x @ w_q) * Dh ** -0.5
    k = heads(x @ w_k)
    v = heads(x @ w_v)
    pad = ((0, 0), (0, 0), (0, sp - S), (0, 0))
    kb = jnp.pad(k, pad).reshape(B, H, nb, L, Dh)
    vb = jnp.pad(v, pad).reshape(B, H, nb, L, Dh)

    pos = jnp.arange(S)
    q_blk = pos // L
    k_mean = jnp.mean(kb, axis=3)
    gate = jnp.einsum("bhsd,bhnd->bhsn", q, k_mean)
    gate = jnp.where(jnp.arange(nb)[None, :] < q_blk[:, None], gate, -jnp.inf)
    k_sel = min(MOBA_TOPK, nb)
    _, sel = lax.top_k(gate, k_sel)
    valid = sel < q_blk[:, None]

    def chunks(t):
        return jnp.moveaxis(t.reshape(B, H, nq, Cq, *t.shape[3:]), 2, 0)

    bi = jnp.arange(B)[:, None, None, None]
    hi = jnp.arange(H)[None, :, None, None]

    def attend(inp):
        qc, selc, validc, c = inp
        tq = c * Cq + jnp.arange(Cq)
        own = (c * Cq) // L
        k_own = lax.dynamic_index_in_dim(kb, own, axis=2, keepdims=False)
        v_own = lax.dynamic_index_in_dim(vb, own, axis=2, keepdims=False)
        kg = kb[bi, hi, selc]
        vg = vb[bi, hi, selc]
        dist_own = (tq[:, None] - (own * L + jnp.arange(L))[None, :]).astype(f32)
        sc_own = jnp.einsum("bhqd,bhsd->bhqs", qc, k_own) - slopes[:, None, None] * dist_own
        sc_own = jnp.where(dist_own >= 0, sc_own, -jnp.inf)
        past_pos = selc[..., None] * L + jnp.arange(L)
        dist_past = (tq[:, None, None] - past_pos).astype(f32)
        sc_past = (jnp.einsum("bhqd,bhqjsd->bhqjs", qc, kg)
                   - slopes[:, None, None, None] * dist_past)
        sc_past = jnp.where(validc[..., None], sc_past, -jnp.inf)
        scores = jnp.concatenate([sc_own, sc_past.reshape(B, H, Cq, k_sel * L)], axis=-1)
        p = jax.nn.softmax(scores, axis=-1)
        out = (jnp.einsum("bhqs,bhsd->bhqd", p[..., :L], v_own)
               + jnp.einsum("bhqjs,bhqjsd->bhqd",
                            p[..., L:].reshape(B, H, Cq, k_sel, L), vg))
        return out

    o = lax.map(attend, (chunks(q), chunks(sel), chunks(valid), jnp.arange(nq)))
    o = o.transpose(1, 0, 3, 2, 4).reshape(B, S, H * Dh).astype(x.dtype)
    return o @ w_o


def setup_inputs(seed: int = 0) -> dict:
    key = jax.random.key(seed)
    ks = iter(jax.random.split(key, 64))
    f32 = jnp.float32

    def w(shape, fan_in):
        return jax.random.normal(next(ks), shape, f32) * fan_in ** -0.5

    def gain(shape):
        return 1.0 + 0.05 * jax.random.normal(next(ks), shape, f32)

    def bias(shape, scale=0.02):
        return scale * jax.random.normal(next(ks), shape, f32)

    D, F = D_MODEL, D_FF
    return {
        "x": jax.random.normal(next(ks), (BATCH, SEQ, D), f32),
        "g_ffn1": gain((DEPTH, D)),
        "g_mix": gain((DEPTH, D)),
        "g_ffn2": gain((DEPTH, D)),
        "g_final": gain((D,)),
        "w1_gate": w((DEPTH, D, F), D),
        "w1_up": w((DEPTH, D, F), D),
        "w1_down": w((DEPTH, F, D), F),
        "w2_gate": w((DEPTH, D, F), D),
        "w2_up": w((DEPTH, D, F), D),
        "w2_down": w((DEPTH, F, D), F),
        "cv_w_pw1": w((N_CONV, D, 2 * D), D),
        "cv_b_pw1": bias((N_CONV, 2 * D)),
        "cv_w_dw": w((N_CONV, CONV_WIDTH, D), CONV_WIDTH),
        "cv_b_dw": bias((N_CONV, D)),
        "cv_ln_g": gain((N_CONV, D)),
        "cv_ln_b": bias((N_CONV, D)),
        "cv_w_pw2": w((N_CONV, D, D), D),
        "cv_b_pw2": bias((N_CONV, D)),
        "gla_w_q": w((N_GLA, D, GLA_DK), D),
        "gla_w_k": w((N_GLA, D, GLA_DK), D),
        "gla_w_v": w((N_GLA, D, GLA_DV), D),
        "gla_w_a1": w((N_GLA, D, GLA_GATE_RANK), D),
        "gla_w_a2": w((N_GLA, GLA_GATE_RANK, GLA_DK), GLA_GATE_RANK),
        "gla_b_a": bias((N_GLA, GLA_DK), 0.1),
        "gla_w_r": w((N_GLA, D, GLA_DV), D),
        "gla_b_r": bias((N_GLA, GLA_DV)),
        "gla_norm_g": gain((N_GLA, GLA_HV)),
        "gla_w_o": w((N_GLA, GLA_DV, D), GLA_DV),
        "mb_w_q": w((N_MOBA, D, MOBA_HEADS * MOBA_HEAD_DIM), D),
        "mb_w_k": w((N_MOBA, D, MOBA_HEADS * MOBA_HEAD_DIM), D),
        "mb_w_v": w((N_MOBA, D, MOBA_HEADS * MOBA_HEAD_DIM), D),
        "mb_w_o": w((N_MOBA, MOBA_HEADS * MOBA_HEAD_DIM, D), MOBA_HEADS * MOBA_HEAD_DIM),
    }


def reference(x, g_ffn1, g_mix, g_ffn2, g_final,
              w1_gate, w1_up, w1_down, w2_gate, w2_up, w2_down,
              cv_w_pw1, cv_b_pw1, cv_w_dw, cv_b_dw, cv_ln_g, cv_ln_b, cv_w_pw2, cv_b_pw2,
              gla_w_q, gla_w_k, gla_w_v, gla_w_a1, gla_w_a2, gla_b_a, gla_w_r, gla_b_r,
              gla_norm_g, gla_w_o,
              mb_w_q, mb_w_k, mb_w_v, mb_w_o):
    h = x
    for i in range(DEPTH):
        h = h + 0.5 * swiglu(rmsnorm(h, g_ffn1[i]), w1_gate[i], w1_up[i], w1_down[i])
        u = rmsnorm(h, g_mix[i])
        kind = i % N_MIXERS
        j = i // N_MIXERS
        if kind == 0:
            m = conformer_conv(u, cv_w_pw1[j], cv_b_pw1[j], cv_w_dw[j], cv_b_dw[j],
                               cv_ln_g[j], cv_ln_b[j], cv_w_pw2[j], cv_b_pw2[j])
        elif kind == 1:
            m = gla_mixer(u, gla_w_q[j], gla_w_k[j], gla_w_v[j], gla_w_a1[j], gla_w_a2[j],
                          gla_b_a[j], gla_w_r[j], gla_b_r[j], gla_norm_g[j], gla_w_o[j])
        else:
            m = moba_mixer(u, mb_w_q[j], mb_w_k[j], mb_w_v[j], mb_w_o[j])
        h = h + m
        h = h + 0.5 * swiglu(rmsnorm(h, g_ffn2[i]), w2_gate[i], w2_up[i], w2_down[i])
    return rmsnorm(h, g_final)
```

```python
import functools

import jax
import jax.numpy as jnp
import numpy as np
from jax import lax
from jax.experimental import pallas as pl
from jax.experimental.pallas import tpu as pltpu

EPS = 1e-6
CONV_WIDTH = 31
GLA_HEADS = 4
GLA_TAU = 16.0
GLA_CHUNK = 64
GLA_LEVELS = 6
GLA_RANK_PAD = 128
MOBA_HEADS = 8
MOBA_BLOCK = 256
MOBA_TOPK = 3

V7X_VMEM_LIMIT_BYTES = 56 * 1024 * 1024
SUBLANES = 8
CONV_HALO = 32

BF16 = jnp.bfloat16
F32 = jnp.float32


def _params(*semantics):
    return pltpu.CompilerParams(dimension_semantics=semantics,
                                vmem_limit_bytes=V7X_VMEM_LIMIT_BYTES)


def _resident(shape):
    return pl.BlockSpec(shape, lambda *_: (0,) * len(shape), pipeline_mode=pl.Buffered(1))


def _rms(x, g):
    return x * lax.rsqrt(jnp.mean(x * x, axis=-1, keepdims=True) + EPS) * g


def _dot(a, b):
    return jnp.dot(a, b, preferred_element_type=F32)


def _dot_nt(a, b):
    return lax.dot_general(a, b, (((1,), (1,)), ((), ())), preferred_element_type=F32)


def _dot_tn(a, b):
    return lax.dot_general(a, b, (((0,), (0,)), ((), ())), preferred_element_type=F32)


def _ffn_body(h_ref, g_ref, wg_ref, wu_ref, wd_ref, gf_ref, o_ref, *, final_norm):
    x = h_ref[...]
    xn = _rms(x, g_ref[...]).astype(BF16)
    gate = _dot(xn, wg_ref[...])
    up = _dot(xn, wu_ref[...])
    a = (gate * jax.nn.sigmoid(gate) * up).astype(BF16)
    y = x + 0.5 * _dot(a, wd_ref[...])
    if final_norm:
        y = _rms(y, gf_ref[...])
    o_ref[...] = y


def _ffn(h, g, wg, wu, wd, g_final, *, final_norm, tm=256):
    s, d = h.shape
    f = wg.shape[1]
    row = pl.BlockSpec((tm, d), lambda i: (i, 0))
    return pl.pallas_call(
        functools.partial(_ffn_body, final_norm=final_norm),
        out_shape=jax.ShapeDtypeStruct((s, d), F32),
        grid=(s // tm,),
        in_specs=[row, _resident((1, d)), _resident((d, f)), _resident((d, f)),
                  _resident((f, d)), _resident((1, d))],
        out_specs=row,
        compiler_params=_params("arbitrary"),
        name="ffn",
    )(h, g, wg, wu, wd, g_final)


def _conv_body(h_ref, g_ref, w1_ref, b1_ref, wdw_ref, bdw_ref, lng_ref, lnb_ref,
               w2_ref, b2_ref, o_ref, ext_ref, cv_ref, *, tm, rb):
    d = h_ref.shape[1]

    @pl.when(pl.program_id(0) == 0)
    def _():
        ext_ref[0:CONV_HALO, :] = jnp.zeros((CONV_HALO, d), F32)

    x = h_ref[...]
    u = _rms(x, g_ref[...]).astype(BF16)
    y = _dot(u, w1_ref[...]) + b1_ref[...]
    ext_ref[CONV_HALO:CONV_HALO + tm, :] = y[:, :d] * jax.nn.sigmoid(y[:, d:])

    shift = CONV_HALO - (CONV_WIDTH - 1)

    def rows(r, carry):
        base = pl.multiple_of(r * rb, rb)
        acc = jnp.broadcast_to(bdw_ref[...], (rb, d))
        for res in range(SUBLANES):
            taps = [j for j in range(CONV_WIDTH) if (shift + j) % SUBLANES == res]
            nrows = rb + (SUBLANES if res else 0)
            part = None
            for j in taps:
                off = pl.multiple_of(base + (shift + j - res), SUBLANES)
                term = wdw_ref[j:j + 1, :] * ext_ref[pl.ds(off, nrows), :]
                part = term if part is None else part + term
            acc = acc + part[res:res + rb]
        cv_ref[pl.ds(base, rb), :] = acc
        return carry

    lax.fori_loop(0, tm // rb, rows, 0)
    ext_ref[0:CONV_HALO, :] = ext_ref[tm:tm + CONV_HALO, :]

    c = cv_ref[...]
    mu = jnp.mean(c, axis=-1, keepdims=True)
    cc = c - mu
    var = jnp.mean(cc * cc, axis=-1, keepdims=True)
    z = cc * lax.rsqrt(var + EPS) * lng_ref[...] + lnb_ref[...]
    z = (z * jax.nn.sigmoid(z)).astype(BF16)
    o_ref[...] = x + _dot(z, w2_ref[...]) + b2_ref[...]


def _conv_mixer(h, g, w1, b1, wdw, bdw, lng, lnb, w2, b2, *, tm=512, rb=32):
    s, d = h.shape
    row = pl.BlockSpec((tm, d), lambda i: (i, 0))
    return pl.pallas_call(
        functools.partial(_conv_body, tm=tm, rb=rb),
        out_shape=jax.ShapeDtypeStruct((s, d), F32),
        grid=(s // tm,),
        in_specs=[row, _resident((1, d)), _resident((d, 2 * d)), _resident((1, 2 * d)),
                  _resident((CONV_WIDTH, d)), _resident((1, d)), _resident((1, d)),
                  _resident((1, d)), _resident((d, d)), _resident((1, d))],
        out_specs=row,
        scratch_shapes=[pltpu.VMEM((tm + CONV_HALO, d), F32), pltpu.VMEM((tm, d), F32)],
        compiler_params=_params("arbitrary"),
        name="conv_mixer",
    )(h, g, w1, b1, wdw, bdw, lng, lnb, w2, b2)


def _gla_tables():
    c = GLA_CHUNK
    r = np.arange(c)[:, None]
    k = np.arange(c)[None, :]
    mats = [(k <= r), (k > r)]
    masks = [(k == r)]
    for lvl in range(1, GLA_LEVELS + 1):
        blk, half = 1 << lvl, 1 << (lvl - 1)
        mid = (r // blk) * blk + half - 1
        second = (r % blk) >= half
        mats.append(np.where(second, (k > mid) & (k <= r), (k > r) & (k <= mid)))
        masks.append(((r // blk) == (k // blk)) & second & ((k % blk) < half))
    return (np.concatenate(mats, 0).astype(np.float32),
            np.concatenate(masks, 0).astype(np.float32))


def _gla_body(h_ref, g_ref, wq_ref, wk_ref, wv_ref, wr_ref, wa1_ref, wa2_ref, ba_ref, br_ref,
              ng_ref, wo_ref, tab_ref, msk_ref, o_ref,
              q_s, k_s, v_s, la_s, r_s, og_s, st_s, *, tm):
    c = GLA_CHUNK
    dk = wq_ref.shape[1]
    dv = wv_ref.shape[1]
    hk = dk // GLA_HEADS
    hv = dv // GLA_HEADS

    @pl.when(pl.program_id(0) == 0)
    def _():
        st_s[...] = jnp.zeros(st_s.shape, F32)

    x = h_ref[...]
    u = _rms(x, g_ref[...]).astype(BF16)
    q_s[...] = _dot(u, wq_ref[...]) * hk ** -0.5
    k_s[...] = _dot(u, wk_ref[...])
    v_s[...] = _dot(u, wv_ref[...]).astype(BF16)
    lo = _dot(u, wa1_ref[...]).astype(BF16)
    la_s[...] = jax.nn.log_sigmoid(_dot(lo, wa2_ref[...]) + ba_ref[...]) / GLA_TAU
    rr = _dot(u, wr_ref[...]) + br_ref[...]
    r_s[...] = rr * jax.nn.sigmoid(rr)

    tab = tab_ref[...]

    def chunk(ci, carry):
        rows = pl.ds(pl.multiple_of(ci * c, c), c)
        la = la_s[rows, :]
        la1 = la.astype(BF16)
        rem = la - la1.astype(F32)
        la2 = rem.astype(BF16)
        la3 = (rem - la2.astype(F32)).astype(BF16)
        e = jnp.exp(_dot(tab, la1) + _dot(tab, la2) + _dot(tab, la3))
        eb = e[0:c]
        ekl = e[c:2 * c]
        elast = e[c - 1:c]
        q = q_s[rows, :]
        k = k_s[rows, :]
        v = v_s[rows, :]
        for hd in range(GLA_HEADS):
            ks = slice(hd * hk, (hd + 1) * hk)
            vs = slice(hd * hv, (hd + 1) * hv)
            qh, kh, vh = q[:, ks], k[:, ks], v[:, vs]
            att = msk_ref[0:c, :] * _dot_nt(qh.astype(BF16), kh.astype(BF16))
            for lvl in range(1, GLA_LEVELS + 1):
                fl = e[(1 + lvl) * c:(2 + lvl) * c, ks]
                att = att + msk_ref[lvl * c:(lvl + 1) * c, :] * _dot_nt(
                    (qh * fl).astype(BF16), (kh * fl).astype(BF16))
            st = st_s[hd]
            o = _dot(att.astype(BF16), vh) + _dot_nt((qh * eb[:, ks]).astype(BF16), st.astype(BF16))
            st_s[hd] = elast[:, ks] * st + _dot_tn(vh, (kh * ekl[:, ks]).astype(BF16))
            on = _rms(o, ng_ref[...])
            og_s[rows, vs] = (on * r_s[rows, vs]).astype(BF16)
        return carry

    lax.fori_loop(0, tm // c, chunk, 0)
    o_ref[...] = x + _dot(og_s[...], wo_ref[...])


def _gla_mixer(h, g, wq, wk, wv, wr, wa1, wa2, ba, br, ng, wo, *, tm=256):
    s, d = h.shape
    dk, dv = wq.shape[1], wv.shape[1]
    tab, msk = _gla_tables()
    tab = jnp.asarray(tab, BF16)
    msk = jnp.asarray(msk, F32)
    row = pl.BlockSpec((tm, d), lambda i: (i, 0))
    return pl.pallas_call(
        functools.partial(_gla_body, tm=tm),
        out_shape=jax.ShapeDtypeStruct((s, d), F32),
        grid=(s // tm,),
        in_specs=[row, _resident((1, d)), _resident(wq.shape), _resident(wk.shape),
                  _resident(wv.shape), _resident(wr.shape), _resident(wa1.shape),
                  _resident(wa2.shape), _resident((1, dk)), _resident((1, dv)),
                  _resident((1, dv // GLA_HEADS)), _resident(wo.shape),
                  _resident(tab.shape), _resident(msk.shape)],
        out_specs=row,
        scratch_shapes=[pltpu.VMEM((tm, dk), F32), pltpu.VMEM((tm, dk), F32),
                        pltpu.VMEM((tm, dv), BF16), pltpu.VMEM((tm, dk), F32),
                        pltpu.VMEM((tm, dv), F32), pltpu.VMEM((tm, dv), BF16),
                        pltpu.VMEM((GLA_HEADS, dv // GLA_HEADS, dk // GLA_HEADS), F32)],
        compiler_params=_params("arbitrary"),
        name="gla_mixer",
    )(h, g, wq, wk, wv, wr, wa1, wa2, ba, br, ng, wo, tab, msk)


def _moba_proj_body(h_ref, g_ref, wq_ref, wk_ref, wv_ref, q_ref, k_ref, v_ref, km_ref, *, scale):
    u = _rms(h_ref[...], g_ref[...]).astype(BF16)
    q_ref[...] = (_dot(u, wq_ref[...]) * scale).astype(BF16)
    k = _dot(u, wk_ref[...])
    k_ref[...] = k.astype(BF16)
    km_ref[0] = jnp.mean(k, axis=0, keepdims=True)
    v_ref[...] = _dot(u, wv_ref[...]).astype(BF16)


def _moba_proj(h, g, wq, wk, wv):
    s, d = h.shape
    tm = MOBA_BLOCK
    nb = s // tm
    row = pl.BlockSpec((tm, d), lambda i: (i, 0))
    return pl.pallas_call(
        functools.partial(_moba_proj_body, scale=(d // MOBA_HEADS) ** -0.5),
        out_shape=[jax.ShapeDtypeStruct((s, d), BF16)] * 3 + [jax.ShapeDtypeStruct((nb, 1, d), F32)],
        grid=(nb,),
        in_specs=[row, _resident((1, d)), _resident((d, d)), _resident((d, d)), _resident((d, d))],
        out_specs=[row, row, row, pl.BlockSpec((1, 1, d), lambda i: (i, 0, 0))],
        compiler_params=_params("arbitrary"),
        name="moba_proj",
    )(h, g, wq, wk, wv)


def _moba_attn_body(slope_ref, q_ref, k_ref, v_ref, km_ref, o_ref, sel_ref, *, nb):
    blk = MOBA_BLOCK
    hd = pl.program_id(0)
    qi = pl.program_id(1)
    slope = slope_ref[hd]
    q = q_ref[...]

    col = lax.broadcasted_iota(jnp.int32, (blk, nb), 1)
    gate = jnp.where(col < qi, _dot_nt(q, km_ref[...].astype(BF16)), -jnp.inf)
    sel = jnp.zeros((blk, nb), F32)
    for _ in range(MOBA_TOPK):
        best = jnp.max(gate, axis=1, keepdims=True)
        idx = jnp.min(jnp.where(gate == best, col, nb), axis=1, keepdims=True)
        hit = col == idx
        sel = jnp.where(hit & (best > -jnp.inf), 1.0, sel)
        gate = jnp.where(hit, -jnp.inf, gate)
    sel_ref[...] = sel

    dist = (lax.broadcasted_iota(jnp.int32, (blk, blk), 0)
            - lax.broadcasted_iota(jnp.int32, (blk, blk), 1))
    distf = dist.astype(F32)
    own = pl.ds(pl.multiple_of(qi * blk, blk), blk)
    s = jnp.where(dist >= 0, _dot_nt(q, k_ref[own, :]) - slope * distf, -jnp.inf)
    m = jnp.max(s, axis=1, keepdims=True)
    p = jnp.exp(s - m)
    l = jnp.sum(p, axis=1, keepdims=True)
    acc = _dot(p.astype(BF16), v_ref[own, :])

    def past(kb, carry):
        m, l, acc = carry
        rows = pl.ds(pl.multiple_of(kb * blk, blk), blk)
        chosen = jnp.sum(jnp.where(col == kb, sel_ref[...], 0.0), axis=1, keepdims=True) > 0.0
        off = ((qi - kb) * blk).astype(F32)
        s = _dot_nt(q, k_ref[rows, :]) - slope * (distf + off)
        s = jnp.where(chosen, s, -jnp.inf)
        m_new = jnp.maximum(m, jnp.max(s, axis=1, keepdims=True))
        alpha = jnp.exp(m - m_new)
        p = jnp.exp(s - m_new)
        l = alpha * l + jnp.sum(p, axis=1, keepdims=True)
        acc = alpha * acc + _dot(p.astype(BF16), v_ref[rows, :])
        return m_new, l, acc

    m, l, acc = lax.fori_loop(0, qi, past, (m, l, acc))
    o_ref[...] = (acc / l).astype(BF16)


def _moba_attn(q, k, v, kmean, slopes):
    s, d = q.shape
    dh = d // MOBA_HEADS
    blk = MOBA_BLOCK
    nb = s // blk
    return pl.pallas_call(
        functools.partial(_moba_attn_body, nb=nb),
        out_shape=jax.ShapeDtypeStruct((s, d), BF16),
        grid_spec=pltpu.PrefetchScalarGridSpec(
            num_scalar_prefetch=1,
            grid=(MOBA_HEADS, nb),
            in_specs=[pl.BlockSpec((blk, dh), lambda h, i, _: (i, h)),
                      pl.BlockSpec((s, dh), lambda h, i, _: (0, h)),
                      pl.BlockSpec((s, dh), lambda h, i, _: (0, h)),
                      pl.BlockSpec((nb, dh), lambda h, i, _: (0, h))],
            out_specs=pl.BlockSpec((blk, dh), lambda h, i, _: (i, h)),
            scratch_shapes=[pltpu.VMEM((blk, nb), F32)]),
        compiler_params=_params("arbitrary", "arbitrary"),
        name="moba_attn",
    )(slopes, q, k, v, kmean)


def _proj_residual_body(h_ref, a_ref, w_ref, o_ref):
    o_ref[...] = h_ref[...] + _dot(a_ref[...], w_ref[...])


def _proj_residual(h, a, w, *, tm=512):
    s, d = h.shape
    row = pl.BlockSpec((tm, d), lambda i: (i, 0))
    return pl.pallas_call(
        _proj_residual_body,
        out_shape=jax.ShapeDtypeStruct((s, d), F32),
        grid=(s // tm,),
        in_specs=[row, pl.BlockSpec((tm, a.shape[1]), lambda i: (i, 0)), _resident(w.shape)],
        out_specs=row,
        compiler_params=_params("arbitrary"),
        name="proj_residual",
    )(h, a, w)


def _moba_mixer(h, g, wq, wk, wv, wo):
    s, d = h.shape
    q, k, v, kmean = _moba_proj(h, g, wq, wk, wv)
    heads = jnp.arange(1, MOBA_HEADS + 1, dtype=F32)
    slopes = 2.0 ** (-8.0 * heads / MOBA_HEADS)
    o = _moba_attn(q, k, v, kmean.reshape(s // MOBA_BLOCK, d), slopes)
    return _proj_residual(h, o, wo)


def kernel(x, g_ffn1, g_mix, g_ffn2, g_final, w1_gate, w1_up, w1_down, w2_gate, w2_up, w2_down, cv_w_pw1, cv_b_pw1, cv_w_dw, cv_b_dw, cv_ln_g, cv_ln_b, cv_w_pw2, cv_b_pw2, gla_w_q, gla_w_k, gla_w_v, gla_w_a1, gla_w_a2, gla_b_a, gla_w_r, gla_b_r, gla_norm_g, gla_w_o, mb_w_q, mb_w_k, mb_w_v, mb_w_o):
    batch, seq, d = x.shape
    depth = g_ffn1.shape[0]
    assert batch == 1
    bf = lambda w: w.astype(BF16)
    vec = lambda b: b.reshape(1, -1)
    h = x.reshape(seq, d)
    gf = vec(g_final)
    for i in range(depth):
        h = _ffn(h, vec(g_ffn1[i]), bf(w1_gate[i]), bf(w1_up[i]), bf(w1_down[i]), gf,
                 final_norm=False)
        kind, j = i % 3, i // 3
        gm = vec(g_mix[i])
        if kind == 0:
            h = _conv_mixer(h, gm, bf(cv_w_pw1[j]), vec(cv_b_pw1[j]), cv_w_dw[j], vec(cv_b_dw[j]),
                            vec(cv_ln_g[j]), vec(cv_ln_b[j]), bf(cv_w_pw2[j]), vec(cv_b_pw2[j]))
        elif kind == 1:
            rank = gla_w_a1.shape[2]
            wa1 = jnp.pad(gla_w_a1[j], ((0, 0), (0, GLA_RANK_PAD - rank)))
            wa2 = jnp.pad(gla_w_a2[j], ((0, GLA_RANK_PAD - rank), (0, 0)))
            h = _gla_mixer(h, gm, bf(gla_w_q[j]), bf(gla_w_k[j]), bf(gla_w_v[j]), bf(gla_w_r[j]),
                           bf(wa1), bf(wa2), vec(gla_b_a[j]), vec(gla_b_r[j]),
                           vec(gla_norm_g[j]), bf(gla_w_o[j]))
        else:
            h = _moba_mixer(h, gm, bf(mb_w_q[j]), bf(mb_w_k[j]), bf(mb_w_v[j]), bf(mb_w_o[j]))
        h = _ffn(h, vec(g_ffn2[i]), bf(w2_gate[i]), bf(w2_up[i]), bf(w2_down[i]), gf,
                 final_norm=(i == depth - 1))
    return h.reshape(batch, seq, d)
```

```python
import functools

import jax
import jax.numpy as jnp
import numpy as np
from jax import lax
from jax.experimental import pallas as pl
from jax.experimental.pallas import tpu as pltpu

EPS = 1e-6
CONV_WIDTH = 31
GLA_HEADS = 4
GLA_TAU = 16.0
GLA_CHUNK = 64
GLA_LEVELS = 6
GLA_RANK_PAD = 128
MOBA_HEADS = 8
MOBA_BLOCK = 256
MOBA_TOPK = 3
MOBA_ONES_ROWS = 16

V7X_VMEM_LIMIT_BYTES = 56 * 1024 * 1024
SUBLANES = 8
CONV_HALO = 32

BF16 = jnp.bfloat16
F32 = jnp.float32


def _params(*semantics):
    return pltpu.CompilerParams(dimension_semantics=semantics,
                                vmem_limit_bytes=V7X_VMEM_LIMIT_BYTES)


def _resident(shape):
    return pl.BlockSpec(shape, lambda *_: (0,) * len(shape), pipeline_mode=pl.Buffered(1))


def _rms(x, g):
    return x * lax.rsqrt(jnp.mean(x * x, axis=-1, keepdims=True) + EPS) * g


def _dot(a, b):
    return jnp.dot(a, b, preferred_element_type=F32)


def _dot_nt(a, b):
    return lax.dot_general(a, b, (((1,), (1,)), ((), ())), preferred_element_type=F32)


def _dot_tn(a, b):
    return lax.dot_general(a, b, (((0,), (0,)), ((), ())), preferred_element_type=F32)


def _ffn_body(h_ref, g_ref, wg_ref, wu_ref, wd_ref, gf_ref, o_ref, *, final_norm):
    x = h_ref[...]
    xn = _rms(x, g_ref[...]).astype(BF16)
    gate = _dot(xn, wg_ref[...])
    up = _dot(xn, wu_ref[...])
    a = (gate * jax.nn.sigmoid(gate) * up).astype(BF16)
    y = x + 0.5 * _dot(a, wd_ref[...])
    if final_norm:
        y = _rms(y, gf_ref[...])
    o_ref[...] = y


def _ffn(h, g, wg, wu, wd, g_final, *, final_norm, tm=256):
    s, d = h.shape
    f = wg.shape[1]
    row = pl.BlockSpec((tm, d), lambda i: (i, 0))
    return pl.pallas_call(
        functools.partial(_ffn_body, final_norm=final_norm),
        out_shape=jax.ShapeDtypeStruct((s, d), F32),
        grid=(s // tm,),
        in_specs=[row, _resident((1, d)), _resident((d, f)), _resident((d, f)),
                  _resident((f, d)), _resident((1, d))],
        out_specs=row,
        compiler_params=_params("arbitrary"),
        name="ffn",
    )(h, g, wg, wu, wd, g_final)


def _conv_body(h_ref, g_ref, w1_ref, b1_ref, wdw_ref, bdw_ref, lng_ref, lnb_ref,
               w2_ref, b2_ref, o_ref, ext_ref, cv_ref, *, tm, rb):
    d = h_ref.shape[1]

    @pl.when(pl.program_id(0) == 0)
    def _():
        ext_ref[0:CONV_HALO, :] = jnp.zeros((CONV_HALO, d), F32)

    x = h_ref[...]
    u = _rms(x, g_ref[...]).astype(BF16)
    y = _dot(u, w1_ref[...]) + b1_ref[...]
    ext_ref[CONV_HALO:CONV_HALO + tm, :] = y[:, :d] * jax.nn.sigmoid(y[:, d:])

    shift = CONV_HALO - (CONV_WIDTH - 1)

    def rows(r, carry):
        base = pl.multiple_of(r * rb, rb)
        acc = jnp.broadcast_to(bdw_ref[...], (rb, d))
        for res in range(SUBLANES):
            taps = [j for j in range(CONV_WIDTH) if (shift + j) % SUBLANES == res]
            nrows = rb + (SUBLANES if res else 0)
            part = None
            for j in taps:
                off = pl.multiple_of(base + (shift + j - res), SUBLANES)
                term = wdw_ref[j:j + 1, :] * ext_ref[pl.ds(off, nrows), :]
                part = term if part is None else part + term
            acc = acc + part[res:res + rb]
        cv_ref[pl.ds(base, rb), :] = acc
        return carry

    lax.fori_loop(0, tm // rb, rows, 0)
    ext_ref[0:CONV_HALO, :] = ext_ref[tm:tm + CONV_HALO, :]

    c = cv_ref[...]
    mu = jnp.mean(c, axis=-1, keepdims=True)
    cc = c - mu
    var = jnp.mean(cc * cc, axis=-1, keepdims=True)
    z = cc * lax.rsqrt(var + EPS) * lng_ref[...] + lnb_ref[...]
    z = (z * jax.nn.sigmoid(z)).astype(BF16)
    o_ref[...] = x + _dot(z, w2_ref[...]) + b2_ref[...]


def _conv_mixer(h, g, w1, b1, wdw, bdw, lng, lnb, w2, b2, *, tm=512, rb=32):
    s, d = h.shape
    row = pl.BlockSpec((tm, d), lambda i: (i, 0))
    return pl.pallas_call(
        functools.partial(_conv_body, tm=tm, rb=rb),
        out_shape=jax.ShapeDtypeStruct((s, d), F32),
        grid=(s // tm,),
        in_specs=[row, _resident((1, d)), _resident((d, 2 * d)), _resident((1, 2 * d)),
                  _resident((CONV_WIDTH, d)), _resident((1, d)), _resident((1, d)),
                  _resident((1, d)), _resident((d, d)), _resident((1, d))],
        out_specs=row,
        scratch_shapes=[pltpu.VMEM((tm + CONV_HALO, d), F32), pltpu.VMEM((tm, d), F32)],
        compiler_params=_params("arbitrary"),
        name="conv_mixer",
    )(h, g, w1, b1, wdw, bdw, lng, lnb, w2, b2)


def _gla_tables():
    c = GLA_CHUNK
    r = np.arange(c)[:, None]
    k = np.arange(c)[None, :]
    mats = [(k <= r), (k > r)]
    masks = [(k == r)]
    for lvl in range(1, GLA_LEVELS + 1):
        blk, half = 1 << lvl, 1 << (lvl - 1)
        mid = (r // blk) * blk + half - 1
        second = (r % blk) >= half
        mats.append(np.where(second, (k > mid) & (k <= r), (k > r) & (k <= mid)))
        masks.append(((r // blk) == (k // blk)) & second & ((k % blk) < half))
    return (np.concatenate(mats, 0).astype(np.float32),
            np.concatenate(masks, 0).astype(np.float32))


def _gla_body(h_ref, g_ref, wq_ref, wk_ref, wv_ref, wr_ref, wa1_ref, wa2_ref, ba_ref, br_ref,
              ng_ref, wo_ref, tab_ref, msk_ref, o_ref,
              q_s, k_s, v_s, la_s, r_s, og_s, st_s, *, tm):
    c = GLA_CHUNK
    dk = wq_ref.shape[1]
    dv = wv_ref.shape[1]
    hk = dk // GLA_HEADS
    hv = dv // GLA_HEADS

    @pl.when(pl.program_id(0) == 0)
    def _():
        st_s[...] = jnp.zeros(st_s.shape, F32)

    x = h_ref[...]
    u = _rms(x, g_ref[...]).astype(BF16)
    q_s[...] = _dot(u, wq_ref[...]) * hk ** -0.5
    k_s[...] = _dot(u, wk_ref[...])
    v_s[...] = _dot(u, wv_ref[...]).astype(BF16)
    lo = _dot(u, wa1_ref[...]).astype(BF16)
    la_s[...] = jax.nn.log_sigmoid(_dot(lo, wa2_ref[...]) + ba_ref[...]) / GLA_TAU
    rr = _dot(u, wr_ref[...]) + br_ref[...]
    r_s[...] = rr * jax.nn.sigmoid(rr)

    tab = tab_ref[...]

    def chunk(ci, carry):
        rows = pl.ds(pl.multiple_of(ci * c, c), c)
        la = la_s[rows, :]
        la1 = la.astype(BF16)
        rem = la - la1.astype(F32)
        la2 = rem.astype(BF16)
        la3 = (rem - la2.astype(F32)).astype(BF16)
        e = jnp.exp(_dot(tab, la1) + _dot(tab, la2) + _dot(tab, la3))
        eb = e[0:c]
        ekl = e[c:2 * c]
        elast = e[c - 1:c]
        q = q_s[rows, :]
        k = k_s[rows, :]
        v = v_s[rows, :]
        for hd in range(GLA_HEADS):
            ks = slice(hd * hk, (hd + 1) * hk)
            vs = slice(hd * hv, (hd + 1) * hv)
            qh, kh, vh = q[:, ks], k[:, ks], v[:, vs]
            att = msk_ref[0:c, :] * _dot_nt(qh.astype(BF16), kh.astype(BF16))
            for lvl in range(1, GLA_LEVELS + 1):
                fl = e[(1 + lvl) * c:(2 + lvl) * c, ks]
                att = att + msk_ref[lvl * c:(lvl + 1) * c, :] * _dot_nt(
                    (qh * fl).astype(BF16), (kh * fl).astype(BF16))
            st = st_s[hd]
            o = _dot(att.astype(BF16), vh) + _dot_nt((qh * eb[:, ks]).astype(BF16), st.astype(BF16))
            st_s[hd] = elast[:, ks] * st + _dot_tn(vh, (kh * ekl[:, ks]).astype(BF16))
            on = _rms(o, ng_ref[...])
            og_s[rows, vs] = (on * r_s[rows, vs]).astype(BF16)
        return carry

    lax.fori_loop(0, tm // c, chunk, 0)
    o_ref[...] = x + _dot(og_s[...], wo_ref[...])


def _gla_mixer(h, g, wq, wk, wv, wr, wa1, wa2, ba, br, ng, wo, *, tm=256):
    s, d = h.shape
    dk, dv = wq.shape[1], wv.shape[1]
    tab, msk = _gla_tables()
    tab = jnp.asarray(tab, BF16)
    msk = jnp.asarray(msk, F32)
    row = pl.BlockSpec((tm, d), lambda i: (i, 0))
    return pl.pallas_call(
        functools.partial(_gla_body, tm=tm),
        out_shape=jax.ShapeDtypeStruct((s, d), F32),
        grid=(s // tm,),
        in_specs=[row, _resident((1, d)), _resident(wq.shape), _resident(wk.shape),
                  _resident(wv.shape), _resident(wr.shape), _resident(wa1.shape),
                  _resident(wa2.shape), _resident((1, dk)), _resident((1, dv)),
                  _resident((1, dv // GLA_HEADS)), _resident(wo.shape),
                  _resident(tab.shape), _resident(msk.shape)],
        out_specs=row,
        scratch_shapes=[pltpu.VMEM((tm, dk), F32), pltpu.VMEM((tm, dk), F32),
                        pltpu.VMEM((tm, dv), BF16), pltpu.VMEM((tm, dk), F32),
                        pltpu.VMEM((tm, dv), F32), pltpu.VMEM((tm, dv), BF16),
                        pltpu.VMEM((GLA_HEADS, dv // GLA_HEADS, dk // GLA_HEADS), F32)],
        compiler_params=_params("arbitrary"),
        name="gla_mixer",
    )(h, g, wq, wk, wv, wr, wa1, wa2, ba, br, ng, wo, tab, msk)


def _moba_proj_body(h_ref, g_ref, wq_ref, wk_ref, wvt_ref, q_ref, k_ref, vt_ref, km_ref, *, scale):
    u = _rms(h_ref[...], g_ref[...]).astype(BF16)
    q_ref[...] = (_dot(u, wq_ref[...]) * scale).astype(BF16)
    k = _dot(u, wk_ref[...])
    k_ref[...] = k.astype(BF16)
    km_ref[0] = jnp.mean(k, axis=0, keepdims=True)
    vt = _dot_nt(wvt_ref[...], u).astype(BF16)
    dh = vt.shape[0] // MOBA_HEADS
    for hd in range(MOBA_HEADS):
        base = hd * (dh + MOBA_ONES_ROWS)
        vt_ref[base:base + dh, :] = vt[hd * dh:(hd + 1) * dh]
        vt_ref[base + dh:base + dh + MOBA_ONES_ROWS, :] = jnp.ones((MOBA_ONES_ROWS, vt.shape[1]), BF16)


def _moba_proj(h, g, wq, wk, wvt):
    s, d = h.shape
    tm = MOBA_BLOCK
    nb = s // tm
    dv = d + MOBA_HEADS * MOBA_ONES_ROWS
    row = pl.BlockSpec((tm, d), lambda i: (i, 0))
    return pl.pallas_call(
        functools.partial(_moba_proj_body, scale=(d // MOBA_HEADS) ** -0.5),
        out_shape=[jax.ShapeDtypeStruct((s, d), BF16), jax.ShapeDtypeStruct((s, d), BF16),
                   jax.ShapeDtypeStruct((dv, s), BF16), jax.ShapeDtypeStruct((nb, 1, d), F32)],
        grid=(nb,),
        in_specs=[row, _resident((1, d)), _resident((d, d)), _resident((d, d)), _resident((d, d))],
        out_specs=[row, row, pl.BlockSpec((dv, tm), lambda i: (0, i)),
                   pl.BlockSpec((1, 1, d), lambda i: (i, 0, 0))],
        compiler_params=_params("arbitrary"),
        name="moba_proj",
    )(h, g, wq, wk, wvt)


def _moba_attn_body(slope_ref, q_ref, k_ref, vt_ref, km_ref, o_ref, sel_ref, bias_ref, *, nb, group, sub):
    blk = MOBA_BLOCK
    span = sub * blk
    hd = pl.program_id(0)
    qi = pl.program_id(1)
    slope = slope_ref[hd]
    q = q_ref[...]

    @pl.when(qi == 0)
    def _():
        r = lax.broadcasted_iota(jnp.int32, (span, blk), 0)
        c = lax.broadcasted_iota(jnp.int32, (span, blk), 1)
        bias_ref[...] = slope * (c - r).astype(F32)

    blk_id = lax.broadcasted_iota(jnp.int32, (nb, blk), 0)
    gate = jnp.where(blk_id < qi, _dot_nt(km_ref[...].astype(BF16), q), -jnp.inf)
    sel = jnp.zeros((nb, blk), F32)
    for _ in range(MOBA_TOPK):
        best = jnp.max(gate, axis=0, keepdims=True)
        idx = jnp.min(jnp.where(gate == best, blk_id, nb), axis=0, keepdims=True)
        hit = blk_id == idx
        sel = jnp.where(hit & (best > -jnp.inf), 1.0, sel)
        gate = jnp.where(hit, -jnp.inf, gate)
    sel_ref[...] = sel

    own = pl.ds(pl.multiple_of(qi * blk, blk), blk)
    causal = (lax.broadcasted_iota(jnp.int32, (blk, blk), 1)
              >= lax.broadcasted_iota(jnp.int32, (blk, blk), 0))
    s = jnp.where(causal, _dot_nt(k_ref[own, :], q) - bias_ref[0:blk, :], -jnp.inf)
    m = jnp.max(s, axis=0, keepdims=True)
    acc = _dot(vt_ref[:, own], jnp.exp(s - m).astype(BF16))

    def update(kb0, carry):
        m, acc = carry
        keys = pl.ds(pl.multiple_of(kb0 * blk, span), span)
        off = slope * ((qi - kb0) * blk).astype(F32)
        s = _dot_nt(k_ref[keys, :], q) - bias_ref[...]
        chosen = [sel_ref[pl.ds(kb0 + g, 1), :] > 0.0 for g in range(sub)]
        top = None
        for g in range(sub):
            mg = jnp.where(chosen[g], jnp.max(s[g * blk:(g + 1) * blk], axis=0, keepdims=True), -jnp.inf)
            top = mg if top is None else jnp.maximum(top, mg)
        m_new = jnp.maximum(m, top - off)
        alpha = jnp.exp(m - m_new)
        p = jnp.concatenate(
            [jnp.exp(s[g * blk:(g + 1) * blk] - jnp.where(chosen[g], m_new + off, jnp.inf))
             for g in range(sub)], axis=0)
        acc = alpha * acc + _dot(vt_ref[:, keys], p.astype(BF16))
        return m_new, acc

    def past(j, carry):
        for t in range(group // sub):
            carry = update(j * group + t * sub, carry)
        return carry

    m, acc = lax.fori_loop(0, (qi + group - 1) // group, past, (m, acc))
    dh = q_ref.shape[1]
    o_ref[...] = (acc[0:dh] / acc[dh:dh + 1]).T.astype(BF16)


def _moba_attn(q, k, vt, kmean, slopes, *, group=8, sub=8):
    s, d = q.shape
    dh = d // MOBA_HEADS
    blk = MOBA_BLOCK
    nb = s // blk
    group = min(group, nb)
    sub = min(sub, group)
    assert nb % group == 0 and group % sub == 0
    return pl.pallas_call(
        functools.partial(_moba_attn_body, nb=nb, group=group, sub=sub),
        out_shape=jax.ShapeDtypeStruct((s, d), BF16),
        grid_spec=pltpu.PrefetchScalarGridSpec(
            num_scalar_prefetch=1,
            grid=(MOBA_HEADS, nb),
            in_specs=[pl.BlockSpec((blk, dh), lambda h, i, _: (i, h)),
                      pl.BlockSpec((s, dh), lambda h, i, _: (0, h)),
                      pl.BlockSpec((dh + MOBA_ONES_ROWS, s), lambda h, i, _: (h, 0)),
                      pl.BlockSpec((nb, dh), lambda h, i, _: (0, h))],
            out_specs=pl.BlockSpec((blk, dh), lambda h, i, _: (i, h)),
            scratch_shapes=[pltpu.VMEM((nb, blk), F32), pltpu.VMEM((sub * blk, blk), F32)]),
        compiler_params=_params("arbitrary", "arbitrary"),
        name="moba_attn",
    )(slopes, q, k, vt, kmean)


def _proj_residual_body(h_ref, a_ref, w_ref, o_ref):
    o_ref[...] = h_ref[...] + _dot(a_ref[...], w_ref[...])


def _proj_residual(h, a, w, *, tm=512):
    s, d = h.shape
    row = pl.BlockSpec((tm, d), lambda i: (i, 0))
    return pl.pallas_call(
        _proj_residual_body,
        out_shape=jax.ShapeDtypeStruct((s, d), F32),
        grid=(s // tm,),
        in_specs=[row, pl.BlockSpec((tm, a.shape[1]), lambda i: (i, 0)), _resident(w.shape)],
        out_specs=row,
        compiler_params=_params("arbitrary"),
        name="proj_residual",
    )(h, a, w)


def _moba_mixer(h, g, wq, wk, wvt, wo):
    s, d = h.shape
    q, k, vt, kmean = _moba_proj(h, g, wq, wk, wvt)
    heads = jnp.arange(1, MOBA_HEADS + 1, dtype=F32)
    slopes = 2.0 ** (-8.0 * heads / MOBA_HEADS)
    o = _moba_attn(q, k, vt, kmean.reshape(s // MOBA_BLOCK, d), slopes)
    return _proj_residual(h, o, wo)


def kernel(x, g_ffn1, g_mix, g_ffn2, g_final, w1_gate, w1_up, w1_down, w2_gate, w2_up, w2_down, cv_w_pw1, cv_b_pw1, cv_w_dw, cv_b_dw, cv_ln_g, cv_ln_b, cv_w_pw2, cv_b_pw2, gla_w_q, gla_w_k, gla_w_v, gla_w_a1, gla_w_a2, gla_b_a, gla_w_r, gla_b_r, gla_norm_g, gla_w_o, mb_w_q, mb_w_k, mb_w_v, mb_w_o):
    batch, seq, d = x.shape
    depth = g_ffn1.shape[0]
    assert batch == 1
    bf = lambda w: w.astype(BF16)
    vec = lambda b: b.reshape(1, -1)
    h = x.reshape(seq, d)
    gf = vec(g_final)
    for i in range(depth):
        h = _ffn(h, vec(g_ffn1[i]), bf(w1_gate[i]), bf(w1_up[i]), bf(w1_down[i]), gf,
                 final_norm=False)
        kind, j = i % 3, i // 3
        gm = vec(g_mix[i])
        if kind == 0:
            h = _conv_mixer(h, gm, bf(cv_w_pw1[j]), vec(cv_b_pw1[j]), cv_w_dw[j], vec(cv_b_dw[j]),
                            vec(cv_ln_g[j]), vec(cv_ln_b[j]), bf(cv_w_pw2[j]), vec(cv_b_pw2[j]))
        elif kind == 1:
            rank = gla_w_a1.shape[2]
            wa1 = jnp.pad(gla_w_a1[j], ((0, 0), (0, GLA_RANK_PAD - rank)))
            wa2 = jnp.pad(gla_w_a2[j], ((0, GLA_RANK_PAD - rank), (0, 0)))
            h = _gla_mixer(h, gm, bf(gla_w_q[j]), bf(gla_w_k[j]), bf(gla_w_v[j]), bf(gla_w_r[j]),
                           bf(wa1), bf(wa2), vec(gla_b_a[j]), vec(gla_b_r[j]),
                           vec(gla_norm_g[j]), bf(gla_w_o[j]))
        else:
            h = _moba_mixer(h, gm, bf(mb_w_q[j]), bf(mb_w_k[j]), bf(mb_w_v[j].T), bf(mb_w_o[j]))
        h = _ffn(h, vec(g_ffn2[i]), bf(w2_gate[i]), bf(w2_up[i]), bf(w2_down[i]), gf,
                 final_norm=(i == depth - 1))
    return h.reshape(batch, seq, d)
```

```python
import functools

import jax
import jax.numpy as jnp
import numpy as np
from jax import lax
from jax.experimental import pallas as pl
from jax.experimental.pallas import tpu as pltpu

EPS = 1e-6
CONV_WIDTH = 31
GLA_HEADS = 4
GLA_TAU = 16.0
GLA_CHUNK = 64
GLA_LEVELS = 6
GLA_RANK_PAD = 128
MOBA_HEADS = 8
MOBA_BLOCK = 256
MOBA_TOPK = 3
MOBA_ONES_ROWS = 16

V7X_VMEM_LIMIT_BYTES = 56 * 1024 * 1024
SUBLANES = 8
CONV_HALO = 32

BF16 = jnp.bfloat16
F32 = jnp.float32


def _params(*semantics):
    return pltpu.CompilerParams(dimension_semantics=semantics,
                                vmem_limit_bytes=V7X_VMEM_LIMIT_BYTES)


def _resident(shape):
    return pl.BlockSpec(shape, lambda *_: (0,) * len(shape), pipeline_mode=pl.Buffered(1))


def _rms(x, g):
    return x * lax.rsqrt(jnp.mean(x * x, axis=-1, keepdims=True) + EPS) * g


def _dot(a, b):
    return jnp.dot(a, b, preferred_element_type=F32)


def _dot_nt(a, b):
    return lax.dot_general(a, b, (((1,), (1,)), ((), ())), preferred_element_type=F32)


def _dot_tn(a, b):
    return lax.dot_general(a, b, (((0,), (0,)), ((), ())), preferred_element_type=F32)


def _ffn_body(h_ref, g_ref, wg_ref, wu_ref, wd_ref, gf_ref, o_ref, *, final_norm):
    x = h_ref[...]
    xn = _rms(x, g_ref[...]).astype(BF16)
    gate = _dot(xn, wg_ref[...])
    up = _dot(xn, wu_ref[...])
    a = (gate * jax.nn.sigmoid(gate) * up).astype(BF16)
    y = x + 0.5 * _dot(a, wd_ref[...])
    if final_norm:
        y = _rms(y, gf_ref[...])
    o_ref[...] = y


def _ffn(h, g, wg, wu, wd, g_final, *, final_norm, tm=256):
    s, d = h.shape
    f = wg.shape[1]
    row = pl.BlockSpec((tm, d), lambda i: (i, 0))
    return pl.pallas_call(
        functools.partial(_ffn_body, final_norm=final_norm),
        out_shape=jax.ShapeDtypeStruct((s, d), F32),
        grid=(s // tm,),
        in_specs=[row, _resident((1, d)), _resident((d, f)), _resident((d, f)),
                  _resident((f, d)), _resident((1, d))],
        out_specs=row,
        compiler_params=_params("arbitrary"),
        name="ffn",
    )(h, g, wg, wu, wd, g_final)


def _conv_body(h_ref, g_ref, w1_ref, b1_ref, wdw_ref, bdw_ref, lng_ref, lnb_ref,
               w2_ref, b2_ref, o_ref, ext_ref, cv_ref, *, tm, rb):
    d = h_ref.shape[1]

    @pl.when(pl.program_id(0) == 0)
    def _():
        ext_ref[0:CONV_HALO, :] = jnp.zeros((CONV_HALO, d), F32)

    x = h_ref[...]
    u = _rms(x, g_ref[...]).astype(BF16)
    y = _dot(u, w1_ref[...]) + b1_ref[...]
    ext_ref[CONV_HALO:CONV_HALO + tm, :] = y[:, :d] * jax.nn.sigmoid(y[:, d:])

    shift = CONV_HALO - (CONV_WIDTH - 1)

    def rows(r, carry):
        base = pl.multiple_of(r * rb, rb)
        acc = jnp.broadcast_to(bdw_ref[...], (rb, d))
        for res in range(SUBLANES):
            taps = [j for j in range(CONV_WIDTH) if (shift + j) % SUBLANES == res]
            nrows = rb + (SUBLANES if res else 0)
            part = None
            for j in taps:
                off = pl.multiple_of(base + (shift + j - res), SUBLANES)
                term = wdw_ref[j:j + 1, :] * ext_ref[pl.ds(off, nrows), :]
                part = term if part is None else part + term
            acc = acc + part[res:res + rb]
        cv_ref[pl.ds(base, rb), :] = acc
        return carry

    lax.fori_loop(0, tm // rb, rows, 0)
    ext_ref[0:CONV_HALO, :] = ext_ref[tm:tm + CONV_HALO, :]

    c = cv_ref[...]
    mu = jnp.mean(c, axis=-1, keepdims=True)
    cc = c - mu
    var = jnp.mean(cc * cc, axis=-1, keepdims=True)
    z = cc * lax.rsqrt(var + EPS) * lng_ref[...] + lnb_ref[...]
    z = (z * jax.nn.sigmoid(z)).astype(BF16)
    o_ref[...] = x + _dot(z, w2_ref[...]) + b2_ref[...]


def _conv_mixer(h, g, w1, b1, wdw, bdw, lng, lnb, w2, b2, *, tm=512, rb=32):
    s, d = h.shape
    row = pl.BlockSpec((tm, d), lambda i: (i, 0))
    return pl.pallas_call(
        functools.partial(_conv_body, tm=tm, rb=rb),
        out_shape=jax.ShapeDtypeStruct((s, d), F32),
        grid=(s // tm,),
        in_specs=[row, _resident((1, d)), _resident((d, 2 * d)), _resident((1, 2 * d)),
                  _resident((CONV_WIDTH, d)), _resident((1, d)), _resident((1, d)),
                  _resident((1, d)), _resident((d, d)), _resident((1, d))],
        out_specs=row,
        scratch_shapes=[pltpu.VMEM((tm + CONV_HALO, d), F32), pltpu.VMEM((tm, d), F32)],
        compiler_params=_params("arbitrary"),
        name="conv_mixer",
    )(h, g, w1, b1, wdw, bdw, lng, lnb, w2, b2)


def _gla_tables():
    c = GLA_CHUNK
    r = np.arange(c)[:, None]
    k = np.arange(c)[None, :]
    mats = [(k <= r), (k > r)]
    masks = [(k == r)]
    for lvl in range(1, GLA_LEVELS + 1):
        blk, half = 1 << lvl, 1 << (lvl - 1)
        mid = (r // blk) * blk + half - 1
        second = (r % blk) >= half
        mats.append(np.where(second, (k > mid) & (k <= r), (k > r) & (k <= mid)))
        masks.append(((r // blk) == (k // blk)) & second & ((k % blk) < half))
    return (np.concatenate(mats, 0).astype(np.float32),
            np.concatenate(masks, 0).astype(np.float32))


def _gla_body(h_ref, g_ref, wq_ref, wk_ref, wv_ref, wr_ref, wa1_ref, wa2_ref, ba_ref, br_ref,
              ng_ref, wo_ref, tab_ref, msk_ref, o_ref,
              q_s, k_s, v_s, la_s, r_s, og_s, st_s, *, tm):
    c = GLA_CHUNK
    dk = wq_ref.shape[1]
    dv = wv_ref.shape[1]
    hk = dk // GLA_HEADS
    hv = dv // GLA_HEADS

    @pl.when(pl.program_id(0) == 0)
    def _():
        st_s[...] = jnp.zeros(st_s.shape, F32)

    x = h_ref[...]
    u = _rms(x, g_ref[...]).astype(BF16)
    q_s[...] = _dot(u, wq_ref[...]) * hk ** -0.5
    k_s[...] = _dot(u, wk_ref[...])
    v_s[...] = _dot(u, wv_ref[...]).astype(BF16)
    lo = _dot(u, wa1_ref[...]).astype(BF16)
    la_s[...] = jax.nn.log_sigmoid(_dot(lo, wa2_ref[...]) + ba_ref[...]) / GLA_TAU
    rr = _dot(u, wr_ref[...]) + br_ref[...]
    r_s[...] = rr * jax.nn.sigmoid(rr)

    tab = tab_ref[...]

    def chunk(ci, carry):
        rows = pl.ds(pl.multiple_of(ci * c, c), c)
        la = la_s[rows, :]
        la1 = la.astype(BF16)
        rem = la - la1.astype(F32)
        la2 = rem.astype(BF16)
        la3 = (rem - la2.astype(F32)).astype(BF16)
        e = jnp.exp(_dot(tab, la1) + _dot(tab, la2) + _dot(tab, la3))
        eb = e[0:c]
        ekl = e[c:2 * c]
        elast = e[c - 1:c]
        q = q_s[rows, :]
        k = k_s[rows, :]
        v = v_s[rows, :]
        for hd in range(GLA_HEADS):
            ks = slice(hd * hk, (hd + 1) * hk)
            vs = slice(hd * hv, (hd + 1) * hv)
            qh, kh, vh = q[:, ks], k[:, ks], v[:, vs]
            att = msk_ref[0:c, :] * _dot_nt(qh.astype(BF16), kh.astype(BF16))
            for lvl in range(1, GLA_LEVELS + 1):
                fl = e[(1 + lvl) * c:(2 + lvl) * c, ks]
                att = att + msk_ref[lvl * c:(lvl + 1) * c, :] * _dot_nt(
                    (qh * fl).astype(BF16), (kh * fl).astype(BF16))
            st = st_s[hd]
            o = _dot(att.astype(BF16), vh) + _dot_nt((qh * eb[:, ks]).astype(BF16), st.astype(BF16))
            st_s[hd] = elast[:, ks] * st + _dot_tn(vh, (kh * ekl[:, ks]).astype(BF16))
            on = _rms(o, ng_ref[...])
            og_s[rows, vs] = (on * r_s[rows, vs]).astype(BF16)
        return carry

    lax.fori_loop(0, tm // c, chunk, 0)
    o_ref[...] = x + _dot(og_s[...], wo_ref[...])


def _gla_mixer(h, g, wq, wk, wv, wr, wa1, wa2, ba, br, ng, wo, *, tm=256):
    s, d = h.shape
    dk, dv = wq.shape[1], wv.shape[1]
    tab, msk = _gla_tables()
    tab = jnp.asarray(tab, BF16)
    msk = jnp.asarray(msk, F32)
    row = pl.BlockSpec((tm, d), lambda i: (i, 0))
    return pl.pallas_call(
        functools.partial(_gla_body, tm=tm),
        out_shape=jax.ShapeDtypeStruct((s, d), F32),
        grid=(s // tm,),
        in_specs=[row, _resident((1, d)), _resident(wq.shape), _resident(wk.shape),
                  _resident(wv.shape), _resident(wr.shape), _resident(wa1.shape),
                  _resident(wa2.shape), _resident((1, dk)), _resident((1, dv)),
                  _resident((1, dv // GLA_HEADS)), _resident(wo.shape),
                  _resident(tab.shape), _resident(msk.shape)],
        out_specs=row,
        scratch_shapes=[pltpu.VMEM((tm, dk), F32), pltpu.VMEM((tm, dk), F32),
                        pltpu.VMEM((tm, dv), BF16), pltpu.VMEM((tm, dk), F32),
                        pltpu.VMEM((tm, dv), F32), pltpu.VMEM((tm, dv), BF16),
                        pltpu.VMEM((GLA_HEADS, dv // GLA_HEADS, dk // GLA_HEADS), F32)],
        compiler_params=_params("arbitrary"),
        name="gla_mixer",
    )(h, g, wq, wk, wv, wr, wa1, wa2, ba, br, ng, wo, tab, msk)


def _moba_proj_body(h_ref, g_ref, wq_ref, wk_ref, wvt_ref, q_ref, k_ref, vt_ref, km_ref, *, scale):
    u = _rms(h_ref[...], g_ref[...]).astype(BF16)
    q_ref[...] = (_dot(u, wq_ref[...]) * scale).astype(BF16)
    k = _dot(u, wk_ref[...])
    k_ref[...] = k.astype(BF16)
    km_ref[0] = jnp.mean(k, axis=0, keepdims=True)
    vt = _dot_nt(wvt_ref[...], u).astype(BF16)
    dh = vt.shape[0] // MOBA_HEADS
    for hd in range(MOBA_HEADS):
        base = hd * (dh + MOBA_ONES_ROWS)
        vt_ref[base:base + dh, :] = vt[hd * dh:(hd + 1) * dh]
        vt_ref[base + dh:base + dh + MOBA_ONES_ROWS, :] = jnp.ones((MOBA_ONES_ROWS, vt.shape[1]), BF16)


def _moba_proj(h, g, wq, wk, wvt):
    s, d = h.shape
    tm = MOBA_BLOCK
    nb = s // tm
    dv = d + MOBA_HEADS * MOBA_ONES_ROWS
    row = pl.BlockSpec((tm, d), lambda i: (i, 0))
    return pl.pallas_call(
        functools.partial(_moba_proj_body, scale=(d // MOBA_HEADS) ** -0.5),
        out_shape=[jax.ShapeDtypeStruct((s, d), BF16), jax.ShapeDtypeStruct((s, d), BF16),
                   jax.ShapeDtypeStruct((dv, s), BF16), jax.ShapeDtypeStruct((nb, 1, d), F32)],
        grid=(nb,),
        in_specs=[row, _resident((1, d)), _resident((d, d)), _resident((d, d)), _resident((d, d))],
        out_specs=[row, row, pl.BlockSpec((dv, tm), lambda i: (0, i)),
                   pl.BlockSpec((1, 1, d), lambda i: (i, 0, 0))],
        compiler_params=_params("arbitrary"),
        name="moba_proj",
    )(h, g, wq, wk, wvt)


def _moba_attn_body(slope_ref, q_ref, k_ref, vt_ref, km_ref, o_ref, sel_ref, bias_ref, sa_ref, sb_ref,
                    *, nb, sub):
    blk = MOBA_BLOCK
    span = sub * blk
    hd = pl.program_id(0)
    qi = pl.program_id(1)
    slope = slope_ref[hd]
    q = q_ref[...]

    @pl.when(qi == 0)
    def _():
        r = lax.broadcasted_iota(jnp.int32, (span, blk), 0)
        c = lax.broadcasted_iota(jnp.int32, (span, blk), 1)
        bias_ref[...] = slope * (c - r).astype(F32)

    blk_id = lax.broadcasted_iota(jnp.int32, (nb, blk), 0)
    gate = jnp.where(blk_id < qi, _dot_nt(km_ref[...].astype(BF16), q), -jnp.inf)
    sel = jnp.zeros((nb, blk), F32)
    for _ in range(MOBA_TOPK):
        best = jnp.max(gate, axis=0, keepdims=True)
        idx = jnp.min(jnp.where(gate == best, blk_id, nb), axis=0, keepdims=True)
        hit = blk_id == idx
        sel = jnp.where(hit & (best > -jnp.inf), 1.0, sel)
        gate = jnp.where(hit, -jnp.inf, gate)
    sel_ref[...] = sel

    own = pl.ds(pl.multiple_of(qi * blk, blk), blk)
    causal = (lax.broadcasted_iota(jnp.int32, (blk, blk), 1)
              >= lax.broadcasted_iota(jnp.int32, (blk, blk), 0))
    s = jnp.where(causal, _dot_nt(k_ref[own, :], q) - bias_ref[0:blk, :], -jnp.inf)
    m = jnp.max(s, axis=0, keepdims=True)
    acc = _dot(vt_ref[:, own], jnp.exp(s - m).astype(BF16))

    def scores(kb0, dst_ref):
        keys = pl.ds(pl.multiple_of(kb0 * blk, span), span)
        dst_ref[...] = _dot_nt(k_ref[keys, :], q) - bias_ref[...]

    def update(kb0, src_ref, carry):
        m, acc = carry
        keys = pl.ds(pl.multiple_of(kb0 * blk, span), span)
        off = slope * ((qi - kb0) * blk).astype(F32)
        chosen = [sel_ref[pl.ds(kb0 + g, 1), :] > 0.0 for g in range(sub)]
        top = None
        for g in range(sub):
            mg = jnp.where(chosen[g], jnp.max(src_ref[g * blk:(g + 1) * blk, :], axis=0, keepdims=True),
                           -jnp.inf)
            top = mg if top is None else jnp.maximum(top, mg)
        m_new = jnp.maximum(m, top - off)
        alpha = jnp.exp(m - m_new)
        p = jnp.concatenate(
            [jnp.exp(src_ref[g * blk:(g + 1) * blk, :]
                     - jnp.where(chosen[g], m_new + off, jnp.inf)).astype(BF16)
             for g in range(sub)], axis=0)
        acc = alpha * acc + _dot(vt_ref[:, keys], p)
        return m_new, acc

    def past(j, carry):
        kb0 = j * (2 * sub)
        scores(kb0 + sub, sb_ref)
        carry = update(kb0, sa_ref, carry)
        scores(jnp.minimum(kb0 + 2 * sub, nb - sub), sa_ref)
        return update(kb0 + sub, sb_ref, carry)

    scores(0, sa_ref)
    m, acc = lax.fori_loop(0, (qi + 2 * sub - 1) // (2 * sub), past, (m, acc))
    dh = q_ref.shape[1]
    o_ref[...] = (acc[0:dh] / acc[dh:dh + 1]).T.astype(BF16)


def _moba_attn(q, k, vt, kmean, slopes, *, sub=4):
    s, d = q.shape
    dh = d // MOBA_HEADS
    blk = MOBA_BLOCK
    nb = s // blk
    sub = min(sub, nb // 2)
    assert nb % (2 * sub) == 0
    stage = pltpu.VMEM((sub * blk, blk), F32)
    return pl.pallas_call(
        functools.partial(_moba_attn_body, nb=nb, sub=sub),
        out_shape=jax.ShapeDtypeStruct((s, d), BF16),
        grid_spec=pltpu.PrefetchScalarGridSpec(
            num_scalar_prefetch=1,
            grid=(MOBA_HEADS, nb),
            in_specs=[pl.BlockSpec((blk, dh), lambda h, i, _: (i, h)),
                      pl.BlockSpec((s, dh), lambda h, i, _: (0, h)),
                      pl.BlockSpec((dh + MOBA_ONES_ROWS, s), lambda h, i, _: (h, 0)),
                      pl.BlockSpec((nb, dh), lambda h, i, _: (0, h))],
            out_specs=pl.BlockSpec((blk, dh), lambda h, i, _: (i, h)),
            scratch_shapes=[pltpu.VMEM((nb, blk), F32), stage, stage, stage]),
        compiler_params=_params("arbitrary", "arbitrary"),
        name="moba_attn",
    )(slopes, q, k, vt, kmean)


def _proj_residual_body(h_ref, a_ref, w_ref, o_ref):
    o_ref[...] = h_ref[...] + _dot(a_ref[...], w_ref[...])


def _proj_residual(h, a, w, *, tm=512):
    s, d = h.shape
    row = pl.BlockSpec((tm, d), lambda i: (i, 0))
    return pl.pallas_call(
        _proj_residual_body,
        out_shape=jax.ShapeDtypeStruct((s, d), F32),
        grid=(s // tm,),
        in_specs=[row, pl.BlockSpec((tm, a.shape[1]), lambda i: (i, 0)), _resident(w.shape)],
        out_specs=row,
        compiler_params=_params("arbitrary"),
        name="proj_residual",
    )(h, a, w)


def _moba_mixer(h, g, wq, wk, wvt, wo):
    s, d = h.shape
    q, k, vt, kmean = _moba_proj(h, g, wq, wk, wvt)
    heads = jnp.arange(1, MOBA_HEADS + 1, dtype=F32)
    slopes = 2.0 ** (-8.0 * heads / MOBA_HEADS)
    o = _moba_attn(q, k, vt, kmean.reshape(s // MOBA_BLOCK, d), slopes)
    return _proj_residual(h, o, wo)


def kernel(x, g_ffn1, g_mix, g_ffn2, g_final, w1_gate, w1_up, w1_down, w2_gate, w2_up, w2_down, cv_w_pw1, cv_b_pw1, cv_w_dw, cv_b_dw, cv_ln_g, cv_ln_b, cv_w_pw2, cv_b_pw2, gla_w_q, gla_w_k, gla_w_v, gla_w_a1, gla_w_a2, gla_b_a, gla_w_r, gla_b_r, gla_norm_g, gla_w_o, mb_w_q, mb_w_k, mb_w_v, mb_w_o):
    batch, seq, d = x.shape
    depth = g_ffn1.shape[0]
    assert batch == 1
    bf = lambda w: w.astype(BF16)
    vec = lambda b: b.reshape(1, -1)
    h = x.reshape(seq, d)
    gf = vec(g_final)
    for i in range(depth):
        h = _ffn(h, vec(g_ffn1[i]), bf(w1_gate[i]), bf(w1_up[i]), bf(w1_down[i]), gf,
                 final_norm=False)
        kind, j = i % 3, i // 3
        gm = vec(g_mix[i])
        if kind == 0:
            h = _conv_mixer(h, gm, bf(cv_w_pw1[j]), vec(cv_b_pw1[j]), cv_w_dw[j], vec(cv_b_dw[j]),
                            vec(cv_ln_g[j]), vec(cv_ln_b[j]), bf(cv_w_pw2[j]), vec(cv_b_pw2[j]))
        elif kind == 1:
            rank = gla_w_a1.shape[2]
            wa1 = jnp.pad(gla_w_a1[j], ((0, 0), (0, GLA_RANK_PAD - rank)))
            wa2 = jnp.pad(gla_w_a2[j], ((0, GLA_RANK_PAD - rank), (0, 0)))
            h = _gla_mixer(h, gm, bf(gla_w_q[j]), bf(gla_w_k[j]), bf(gla_w_v[j]), bf(gla_w_r[j]),
                           bf(wa1), bf(wa2), vec(gla_b_a[j]), vec(gla_b_r[j]),
                           vec(gla_norm_g[j]), bf(gla_w_o[j]))
        else:
            h = _moba_mixer(h, gm, bf(mb_w_q[j]), bf(mb_w_k[j]), bf(mb_w_v[j].T), bf(mb_w_o[j]))
        h = _ffn(h, vec(g_ffn2[i]), bf(w2_gate[i]), bf(w2_up[i]), bf(w2_down[i]), gf,
                 final_norm=(i == depth - 1))
    return h.reshape(batch, seq, d)
```

```python
import functools

import jax
import jax.numpy as jnp
import numpy as np
from jax import lax
from jax.experimental import pallas as pl
from jax.experimental.pallas import tpu as pltpu

EPS = 1e-6
CONV_WIDTH = 31
GLA_HEADS = 4
GLA_TAU = 16.0
GLA_CHUNK = 64
GLA_LEVELS = 6
GLA_RANK_PAD = 128
MOBA_HEADS = 8
MOBA_BLOCK = 256
MOBA_TOPK = 3
MOBA_ONES_ROWS = 16
MOBA_MIN_SCORE = float(np.finfo(np.float32).min) / 2

V7X_VMEM_LIMIT_BYTES = 56 * 1024 * 1024
SUBLANES = 8
CONV_HALO = 32

BF16 = jnp.bfloat16
F32 = jnp.float32


def _params(*semantics):
    return pltpu.CompilerParams(dimension_semantics=semantics,
                                vmem_limit_bytes=V7X_VMEM_LIMIT_BYTES)


def _resident(shape):
    return pl.BlockSpec(shape, lambda *_: (0,) * len(shape), pipeline_mode=pl.Buffered(1))


def _rms(x, g):
    return x * lax.rsqrt(jnp.mean(x * x, axis=-1, keepdims=True) + EPS) * g


def _dot(a, b):
    return jnp.dot(a, b, preferred_element_type=F32)


def _dot_nt(a, b):
    return lax.dot_general(a, b, (((1,), (1,)), ((), ())), preferred_element_type=F32)


def _dot_tn(a, b):
    return lax.dot_general(a, b, (((0,), (0,)), ((), ())), preferred_element_type=F32)


def _ffn_body(h_ref, g_ref, wg_ref, wu_ref, wd_ref, gf_ref, o_ref, *, final_norm):
    x = h_ref[...]
    xn = _rms(x, g_ref[...]).astype(BF16)
    gate = _dot(xn, wg_ref[...])
    up = _dot(xn, wu_ref[...])
    a = (gate * jax.nn.sigmoid(gate) * up).astype(BF16)
    y = x + 0.5 * _dot(a, wd_ref[...])
    if final_norm:
        y = _rms(y, gf_ref[...])
    o_ref[...] = y


def _ffn(h, g, wg, wu, wd, g_final, *, layer, final_norm, tm=256):
    s, d = h.shape
    f = wg.shape[2]
    row = pl.BlockSpec((tm, d), lambda i: (i, 0))

    def one_layer(rows, cols):
        return pl.BlockSpec((None, rows, cols), lambda i: (layer, 0, 0), pipeline_mode=pl.Buffered(1))

    return pl.pallas_call(
        functools.partial(_ffn_body, final_norm=final_norm),
        out_shape=jax.ShapeDtypeStruct((s, d), F32),
        grid=(s // tm,),
        in_specs=[row, _resident((1, d)), one_layer(d, f), one_layer(d, f),
                  one_layer(f, d), _resident((1, d))],
        out_specs=row,
        compiler_params=_params("arbitrary"),
        name="ffn",
    )(h, g, wg, wu, wd, g_final)


CAST_BLOCK_BYTES = 6 * 1024 * 1024


def _cast_body(x_ref, o_ref):
    o_ref[...] = x_ref[...].astype(BF16)


def _to_bf16(w):
    layers, rows, cols = w.shape
    fits = [t for t in range(16, rows + 1, 16) if rows % t == 0 and t * cols * 4 <= CAST_BLOCK_BYTES]
    tr = max(fits) if fits else rows
    blk = pl.BlockSpec((1, tr, cols), lambda i, j: (i, j, 0))
    return pl.pallas_call(
        _cast_body,
        out_shape=jax.ShapeDtypeStruct(w.shape, BF16),
        grid=(layers, rows // tr),
        in_specs=[blk],
        out_specs=blk,
        compiler_params=_params("arbitrary", "arbitrary"),
        name="to_bf16",
    )(w)


def _conv_body(h_ref, g_ref, w1_ref, b1_ref, wdw_ref, bdw_ref, lng_ref, lnb_ref,
               w2_ref, b2_ref, o_ref, ext_ref, cv_ref, wb_ref, *, tm, rb, lc):
    d = h_ref.shape[1]

    @pl.when(pl.program_id(0) == 0)
    def _():
        ext_ref[0:CONV_HALO, :] = jnp.zeros((CONV_HALO, d), F32)
        for j in range(CONV_WIDTH):
            wb_ref[j] = jnp.broadcast_to(wdw_ref[j:j + 1, :], (SUBLANES, d))

    x = h_ref[...]
    u = _rms(x, g_ref[...]).astype(BF16)
    y = _dot(u, w1_ref[...]) + b1_ref[...]
    ext_ref[CONV_HALO:CONV_HALO + tm, :] = y[:, :d] * jax.nn.sigmoid(y[:, d:])

    shift = CONV_HALO - (CONV_WIDTH - 1)

    def rows(r, carry):
        base = pl.multiple_of(r * rb, rb)
        for c0 in range(0, d, lc):
            lanes = slice(c0, c0 + lc)
            acc = jnp.broadcast_to(bdw_ref[:, lanes], (rb, lc))
            for res in range(SUBLANES):
                taps = [j for j in range(CONV_WIDTH) if (shift + j) % SUBLANES == res]
                nrows = rb + (SUBLANES if res else 0)
                part = None
                for j in taps:
                    off = pl.multiple_of(base + (shift + j - res), SUBLANES)
                    slabs = ext_ref[pl.ds(off, nrows), lanes].reshape(nrows // SUBLANES, SUBLANES, lc)
                    term = wb_ref[j, :, lanes] * slabs
                    part = term if part is None else part + term
                acc = acc + part.reshape(nrows, lc)[res:res + rb]
            cv_ref[pl.ds(base, rb), lanes] = acc
        return carry

    lax.fori_loop(0, tm // rb, rows, 0)
    ext_ref[0:CONV_HALO, :] = ext_ref[tm:tm + CONV_HALO, :]

    c = cv_ref[...]
    mu = jnp.mean(c, axis=-1, keepdims=True)
    cc = c - mu
    var = jnp.mean(cc * cc, axis=-1, keepdims=True)
    z = cc * lax.rsqrt(var + EPS) * lng_ref[...] + lnb_ref[...]
    z = (z * jax.nn.sigmoid(z)).astype(BF16)
    o_ref[...] = x + _dot(z, w2_ref[...]) + b2_ref[...]


def _conv_mixer(h, g, w1, b1, wdw, bdw, lng, lnb, w2, b2, *, tm=512, rb=64, lc=128):
    s, d = h.shape
    row = pl.BlockSpec((tm, d), lambda i: (i, 0))
    return pl.pallas_call(
        functools.partial(_conv_body, tm=tm, rb=rb, lc=lc),
        out_shape=jax.ShapeDtypeStruct((s, d), F32),
        grid=(s // tm,),
        in_specs=[row, _resident((1, d)), _resident((d, 2 * d)), _resident((1, 2 * d)),
                  _resident((CONV_WIDTH, d)), _resident((1, d)), _resident((1, d)),
                  _resident((1, d)), _resident((d, d)), _resident((1, d))],
        out_specs=row,
        scratch_shapes=[pltpu.VMEM((tm + CONV_HALO, d), F32), pltpu.VMEM((tm, d), F32),
                        pltpu.VMEM((CONV_WIDTH, SUBLANES, d), F32)],
        compiler_params=_params("arbitrary"),
        name="conv_mixer",
    )(h, g, w1, b1, wdw, bdw, lng, lnb, w2, b2)


def _gla_tables():
    c = GLA_CHUNK
    r = np.arange(c)[:, None]
    k = np.arange(c)[None, :]
    mats = [(k <= r), (k > r)]
    masks = [(k == r)]
    for lvl in range(1, GLA_LEVELS + 1):
        blk, half = 1 << lvl, 1 << (lvl - 1)
        mid = (r // blk) * blk + half - 1
        second = (r % blk) >= half
        mats.append(np.where(second, (k > mid) & (k <= r), (k > r) & (k <= mid)))
        masks.append(((r // blk) == (k // blk)) & second & ((k % blk) < half))
    return (np.concatenate(mats, 0).astype(np.float32),
            np.concatenate(masks, 0).astype(np.float32))


def _gla_body(h_ref, g_ref, wq_ref, wk_ref, wv_ref, wr_ref, wa1_ref, wa2_ref, ba_ref, br_ref,
              ng_ref, wo_ref, tab_ref, msk_ref, o_ref,
              q_s, k_s, v_s, la_s, r_s, og_s, st_s, *, tm):
    c = GLA_CHUNK
    dk = wq_ref.shape[1]
    dv = wv_ref.shape[1]
    hk = dk // GLA_HEADS
    hv = dv // GLA_HEADS

    @pl.when(pl.program_id(0) == 0)
    def _():
        st_s[...] = jnp.zeros(st_s.shape, F32)

    x = h_ref[...]
    u = _rms(x, g_ref[...]).astype(BF16)
    q_s[...] = _dot(u, wq_ref[...]) * hk ** -0.5
    k_s[...] = _dot(u, wk_ref[...])
    v_s[...] = _dot(u, wv_ref[...]).astype(BF16)
    lo = _dot(u, wa1_ref[...]).astype(BF16)
    la_s[...] = jax.nn.log_sigmoid(_dot(lo, wa2_ref[...]) + ba_ref[...]) / GLA_TAU
    rr = _dot(u, wr_ref[...]) + br_ref[...]
    r_s[...] = rr * jax.nn.sigmoid(rr)

    tab = tab_ref[...]

    def chunk(ci, carry):
        rows = pl.ds(pl.multiple_of(ci * c, c), c)
        la = la_s[rows, :]
        la1 = la.astype(BF16)
        rem = la - la1.astype(F32)
        la2 = rem.astype(BF16)
        la3 = (rem - la2.astype(F32)).astype(BF16)
        e = jnp.exp(_dot(tab, jnp.concatenate([la1, la2, la3], axis=0)))
        eb = e[0:c]
        ekl = e[c:2 * c]
        elast = e[c - 1:c]
        q = q_s[rows, :]
        k = k_s[rows, :]
        v = v_s[rows, :]
        for hd in range(GLA_HEADS):
            ks = slice(hd * hk, (hd + 1) * hk)
            vs = slice(hd * hv, (hd + 1) * hv)
            qh, kh, vh = q[:, ks], k[:, ks], v[:, vs]
            att = msk_ref[0:c, :] * _dot_nt(qh.astype(BF16), kh.astype(BF16))
            for lvl in range(1, GLA_LEVELS + 1):
                fl = e[(1 + lvl) * c:(2 + lvl) * c, ks]
                att = att + msk_ref[lvl * c:(lvl + 1) * c, :] * _dot_nt(
                    (qh * fl).astype(BF16), (kh * fl).astype(BF16))
            st = st_s[hd]
            o = _dot(att.astype(BF16), vh) + _dot_nt((qh * eb[:, ks]).astype(BF16), st.astype(BF16))
            st_s[hd] = elast[:, ks] * st + _dot_tn(vh, (kh * ekl[:, ks]).astype(BF16))
            on = _rms(o, ng_ref[...])
            og_s[rows, vs] = (on * r_s[rows, vs]).astype(BF16)
        return carry

    lax.fori_loop(0, tm // c, chunk, 0, unroll=True)
    o_ref[...] = x + _dot(og_s[...], wo_ref[...])


def _gla_mixer(h, g, wq, wk, wv, wr, wa1, wa2, ba, br, ng, wo, *, tm=256):
    s, d = h.shape
    dk, dv = wq.shape[1], wv.shape[1]
    tab, msk = _gla_tables()
    tab = jnp.asarray(np.concatenate([tab, tab, tab], axis=1), BF16)
    msk = jnp.asarray(msk, F32)
    row = pl.BlockSpec((tm, d), lambda i: (i, 0))
    return pl.pallas_call(
        functools.partial(_gla_body, tm=tm),
        out_shape=jax.ShapeDtypeStruct((s, d), F32),
        grid=(s // tm,),
        in_specs=[row, _resident((1, d)), _resident(wq.shape), _resident(wk.shape),
                  _resident(wv.shape), _resident(wr.shape), _resident(wa1.shape),
                  _resident(wa2.shape), _resident((1, dk)), _resident((1, dv)),
                  _resident((1, dv // GLA_HEADS)), _resident(wo.shape),
                  _resident(tab.shape), _resident(msk.shape)],
        out_specs=row,
        scratch_shapes=[pltpu.VMEM((tm, dk), F32), pltpu.VMEM((tm, dk), F32),
                        pltpu.VMEM((tm, dv), BF16), pltpu.VMEM((tm, dk), F32),
                        pltpu.VMEM((tm, dv), F32), pltpu.VMEM((tm, dv), BF16),
                        pltpu.VMEM((GLA_HEADS, dv // GLA_HEADS, dk // GLA_HEADS), F32)],
        compiler_params=_params("arbitrary"),
        name="gla_mixer",
    )(h, g, wq, wk, wv, wr, wa1, wa2, ba, br, ng, wo, tab, msk)


def _moba_proj_body(h_ref, g_ref, wq_ref, wk_ref, wvt_ref, q_ref, k_ref, vt_ref, sel_ref, km_s,
                    *, scale, nb):
    i = pl.program_id(0)
    blk, d = h_ref.shape
    dh = d // MOBA_HEADS

    @pl.when(i == 0)
    def _():
        km_s[...] = jnp.zeros(km_s.shape, F32)

    u = _rms(h_ref[...], g_ref[...]).astype(BF16)
    q = (_dot(u, wq_ref[...]) * scale).astype(BF16)
    q_ref[...] = q
    k = _dot(u, wk_ref[...])
    k_ref[...] = k.astype(BF16)
    vt = _dot_nt(wvt_ref[...], u).astype(BF16)
    for hd in range(MOBA_HEADS):
        base = hd * (dh + MOBA_ONES_ROWS)
        vt_ref[base:base + dh, :] = vt[hd * dh:(hd + 1) * dh]
        vt_ref[base + dh:base + dh + MOBA_ONES_ROWS, :] = jnp.ones((MOBA_ONES_ROWS, blk), BF16)

    blk_id = lax.broadcasted_iota(jnp.int32, (nb, blk), 0)
    kmb = km_s[...].astype(BF16)
    for hd in range(MOBA_HEADS):
        cols = slice(hd * dh, (hd + 1) * dh)
        gate = jnp.where(blk_id < i, _dot_nt(kmb[:, cols], q[:, cols]), -jnp.inf)
        sel = jnp.where(blk_id == i, 1.0, 0.0)
        for _ in range(MOBA_TOPK):
            best = jnp.max(gate, axis=0, keepdims=True)
            idx = jnp.min(jnp.where(gate == best, blk_id, nb), axis=0, keepdims=True)
            hit = blk_id == idx
            sel = jnp.where(hit & (best > -jnp.inf), 1.0, sel)
            gate = jnp.where(hit, -jnp.inf, gate)
        sel_ref[0, hd * nb:(hd + 1) * nb, :] = sel
    km_s[pl.ds(i, 1), :] = jnp.mean(k, axis=0, keepdims=True)


def _moba_proj(h, g, wq, wk, wvt):
    s, d = h.shape
    tm = MOBA_BLOCK
    nb = s // tm
    dv = d + MOBA_HEADS * MOBA_ONES_ROWS
    row = pl.BlockSpec((tm, d), lambda i: (i, 0))
    return pl.pallas_call(
        functools.partial(_moba_proj_body, scale=(d // MOBA_HEADS) ** -0.5, nb=nb),
        out_shape=[jax.ShapeDtypeStruct((s, d), BF16), jax.ShapeDtypeStruct((s, d), BF16),
                   jax.ShapeDtypeStruct((dv, s), BF16),
                   jax.ShapeDtypeStruct((nb, MOBA_HEADS * nb, tm), F32)],
        grid=(nb,),
        in_specs=[row, _resident((1, d)), _resident((d, d)), _resident((d, d)), _resident((d, d))],
        out_specs=[row, row, pl.BlockSpec((dv, tm), lambda i: (0, i)),
                   pl.BlockSpec((1, MOBA_HEADS * nb, tm), lambda i: (i, 0, 0))],
        scratch_shapes=[pltpu.VMEM((nb, d), F32)],
        compiler_params=_params("arbitrary"),
        name="moba_proj",
    )(h, g, wq, wk, wvt)


def _moba_attn_body(slope_ref, q_ref, k_ref, vt_ref, sel_ref, o_ref,
                    bias_ref, sa_ref, sb_ref, pa_ref, pb_ref, *, nb, sub):
    blk = MOBA_BLOCK
    span = sub * blk
    hd = pl.program_id(0)
    qi = pl.program_id(1)
    slope = slope_ref[hd]
    q = q_ref[...]
    dh = q.shape[1]

    @pl.when(qi == 0)
    def _():
        r = lax.broadcasted_iota(jnp.int32, (blk, blk), 0)
        c = lax.broadcasted_iota(jnp.int32, (blk, blk), 1)
        rel = (c - r).astype(F32)
        for g in range(sub):
            plain = slope * (rel - float(g * blk))
            bias_ref[g] = plain
            bias_ref[sub + g] = jnp.where(c >= r, plain, jnp.inf)

    def keys_of(kb):
        return pl.ds(pl.multiple_of(kb * blk, span), span)

    def scores(kb, dst_ref):
        raw = _dot_nt(k_ref[keys_of(kb), :], q)
        for g in range(sub):
            table = bias_ref[jnp.where(kb + g == qi, sub + g, g)]
            dst_ref[g * blk:(g + 1) * blk, :] = raw[g * blk:(g + 1) * blk] - table

    def softmax(kb, src_ref, dst_ref, m):
        off = slope * ((qi - kb) * blk).astype(F32)
        chosen = [sel_ref[0, pl.ds(kb + g, 1), :] > 0.0 for g in range(sub)]
        top = None
        for g in range(sub):
            mg = jnp.where(chosen[g], jnp.max(src_ref[g * blk:(g + 1) * blk, :], axis=0, keepdims=True),
                           -jnp.inf)
            top = mg if top is None else jnp.maximum(top, mg)
        m_new = jnp.maximum(m, top - off)
        for g in range(sub):
            shift = jnp.where(chosen[g], m_new + off, jnp.inf)
            dst_ref[g * blk:(g + 1) * blk, :] = jnp.exp(src_ref[g * blk:(g + 1) * blk, :] - shift).astype(BF16)
        return m_new, jnp.exp(m - m_new)

    def values(kb, p_ref):
        return _dot(vt_ref[:, keys_of(kb)], p_ref[...])

    def trip(j, carry):
        m, acc = carry
        kb = j * (2 * sub)
        scores(kb + sub, sb_ref)
        m, alpha = softmax(kb, sa_ref, pa_ref, m)
        acc = alpha * (acc + values(jnp.maximum(kb - sub, 0), pb_ref))
        scores(jnp.minimum(kb + 2 * sub, nb - sub), sa_ref)
        m, alpha = softmax(kb + sub, sb_ref, pb_ref, m)
        acc = alpha * (acc + values(kb, pa_ref))
        return m, acc

    trips = (qi + 2 * sub) // (2 * sub)
    pb_ref[...] = jnp.zeros(pb_ref.shape, BF16)
    scores(0, sa_ref)
    m0 = jnp.full((1, blk), MOBA_MIN_SCORE, F32)
    acc0 = jnp.zeros((dh + MOBA_ONES_ROWS, blk), F32)
    m, acc = lax.fori_loop(0, trips, trip, (m0, acc0))
    acc = acc + values((trips - 1) * (2 * sub) + sub, pb_ref)
    o_ref[...] = (acc[0:dh] / acc[dh:dh + 1]).T.astype(BF16)


def _moba_attn(q, k, vt, sel, slopes, *, sub=4):
    s, d = q.shape
    dh = d // MOBA_HEADS
    blk = MOBA_BLOCK
    nb = s // blk
    sub = min(sub, nb // 2)
    assert nb % (2 * sub) == 0
    scores = pltpu.VMEM((sub * blk, blk), F32)
    probs = pltpu.VMEM((sub * blk, blk), BF16)
    return pl.pallas_call(
        functools.partial(_moba_attn_body, nb=nb, sub=sub),
        out_shape=jax.ShapeDtypeStruct((s, d), BF16),
        grid_spec=pltpu.PrefetchScalarGridSpec(
            num_scalar_prefetch=1,
            grid=(MOBA_HEADS, nb),
            in_specs=[pl.BlockSpec((blk, dh), lambda h, i, _: (i, h)),
                      pl.BlockSpec((s, dh), lambda h, i, _: (0, h)),
                      pl.BlockSpec((dh + MOBA_ONES_ROWS, s), lambda h, i, _: (h, 0)),
                      pl.BlockSpec((1, nb, blk), lambda h, i, _: (i, h, 0))],
            out_specs=pl.BlockSpec((blk, dh), lambda h, i, _: (i, h)),
            scratch_shapes=[pltpu.VMEM((2 * sub, blk, blk), F32), scores, scores, probs, probs]),
        compiler_params=_params("arbitrary", "arbitrary"),
        name="moba_attn",
    )(slopes, q, k, vt, sel)


def _proj_residual_body(h_ref, a_ref, w_ref, o_ref):
    o_ref[...] = h_ref[...] + _dot(a_ref[...], w_ref[...])


def _proj_residual(h, a, w, *, tm=512):
    s, d = h.shape
    row = pl.BlockSpec((tm, d), lambda i: (i, 0))
    return pl.pallas_call(
        _proj_residual_body,
        out_shape=jax.ShapeDtypeStruct((s, d), F32),
        grid=(s // tm,),
        in_specs=[row, pl.BlockSpec((tm, a.shape[1]), lambda i: (i, 0)), _resident(w.shape)],
        out_specs=row,
        compiler_params=_params("arbitrary"),
        name="proj_residual",
    )(h, a, w)


def _moba_mixer(h, g, wq, wk, wvt, wo):
    s, d = h.shape
    q, k, vt, sel = _moba_proj(h, g, wq, wk, wvt)
    heads = jnp.arange(1, MOBA_HEADS + 1, dtype=F32)
    slopes = 2.0 ** (-8.0 * heads / MOBA_HEADS)
    o = _moba_attn(q, k, vt, sel, slopes)
    return _proj_residual(h, o, wo)


def kernel(x, g_ffn1, g_mix, g_ffn2, g_final, w1_gate, w1_up, w1_down, w2_gate, w2_up, w2_down, cv_w_pw1, cv_b_pw1, cv_w_dw, cv_b_dw, cv_ln_g, cv_ln_b, cv_w_pw2, cv_b_pw2, gla_w_q, gla_w_k, gla_w_v, gla_w_a1, gla_w_a2, gla_b_a, gla_w_r, gla_b_r, gla_norm_g, gla_w_o, mb_w_q, mb_w_k, mb_w_v, mb_w_o):
    batch, seq, d = x.shape
    depth = g_ffn1.shape[0]
    assert batch == 1
    bf = lambda w: w.astype(BF16)
    vec = lambda b: b.reshape(1, -1)
    h = x.reshape(seq, d)
    gf = vec(g_final)
    ffn1 = [_to_bf16(w) for w in (w1_gate, w1_up, w1_down)]
    ffn2 = [_to_bf16(w) for w in (w2_gate, w2_up, w2_down)]
    cv_pw1, cv_pw2 = _to_bf16(cv_w_pw1), _to_bf16(cv_w_pw2)
    gla_q, gla_k, gla_v, gla_r, gla_o = [
        _to_bf16(w) for w in (gla_w_q, gla_w_k, gla_w_v, gla_w_r, gla_w_o)]
    mb_q, mb_k, mb_o = [_to_bf16(w) for w in (mb_w_q, mb_w_k, mb_w_o)]
    for i in range(depth):
        h = _ffn(h, vec(g_ffn1[i]), *ffn1, gf, layer=i, final_norm=False)
        kind, j = i % 3, i // 3
        gm = vec(g_mix[i])
        if kind == 0:
            h = _conv_mixer(h, gm, cv_pw1[j], vec(cv_b_pw1[j]), cv_w_dw[j], vec(cv_b_dw[j]),
                            vec(cv_ln_g[j]), vec(cv_ln_b[j]), cv_pw2[j], vec(cv_b_pw2[j]))
        elif kind == 1:
            rank = gla_w_a1.shape[2]
            wa1 = jnp.pad(gla_w_a1[j], ((0, 0), (0, GLA_RANK_PAD - rank)))
            wa2 = jnp.pad(gla_w_a2[j], ((0, GLA_RANK_PAD - rank), (0, 0)))
            h = _gla_mixer(h, gm, gla_q[j], gla_k[j], gla_v[j], gla_r[j],
                           bf(wa1), bf(wa2), vec(gla_b_a[j]), vec(gla_b_r[j]),
                           vec(gla_norm_g[j]), gla_o[j])
        else:
            h = _moba_mixer(h, gm, mb_q[j], mb_k[j], bf(mb_w_v[j].T), mb_o[j])
        h = _ffn(h, vec(g_ffn2[i]), *ffn2, gf, layer=i, final_norm=(i == depth - 1))
    return h.reshape(batch, seq, d)
```

```python
import functools

import jax
import jax.numpy as jnp
import numpy as np
from jax import lax
from jax.experimental import pallas as pl
from jax.experimental.pallas import tpu as pltpu

EPS = 1e-6
CONV_WIDTH = 31
GLA_HEADS = 4
GLA_TAU = 16.0
GLA_CHUNK = 64
GLA_LEVELS = 6
GLA_RANK_PAD = 128
MOBA_HEADS = 8
MOBA_BLOCK = 256
MOBA_TOPK = 3
MOBA_ONES_ROWS = 16
MOBA_MIN_SCORE = float(np.finfo(np.float32).min) / 2

V7X_VMEM_LIMIT_BYTES = 56 * 1024 * 1024
SUBLANES = 8
CONV_HALO = 32

BF16 = jnp.bfloat16
F32 = jnp.float32


def _params(*semantics):
    return pltpu.CompilerParams(dimension_semantics=semantics,
                                vmem_limit_bytes=V7X_VMEM_LIMIT_BYTES)


def _resident(shape):
    return pl.BlockSpec(shape, lambda *_: (0,) * len(shape), pipeline_mode=pl.Buffered(1))


def _rms(x, g):
    return x * lax.rsqrt(jnp.mean(x * x, axis=-1, keepdims=True) + EPS) * g


def _dot(a, b):
    return jnp.dot(a, b, preferred_element_type=F32)


def _dot_nt(a, b):
    return lax.dot_general(a, b, (((1,), (1,)), ((), ())), preferred_element_type=F32)


def _dot_tn(a, b):
    return lax.dot_general(a, b, (((0,), (0,)), ((), ())), preferred_element_type=F32)


def _ffn_body(h_ref, g_ref, wg_ref, wu_ref, wd_ref, gf_ref, o_ref, *, final_norm):
    x = h_ref[...]
    xn = _rms(x, g_ref[...]).astype(BF16)
    gate = _dot(xn, wg_ref[...])
    up = _dot(xn, wu_ref[...])
    a = (gate * jax.nn.sigmoid(gate) * up).astype(BF16)
    y = x + 0.5 * _dot(a, wd_ref[...])
    if final_norm:
        y = _rms(y, gf_ref[...])
    o_ref[...] = y


def _ffn(h, g, wg, wu, wd, g_final, *, layer, final_norm, tm=256):
    s, d = h.shape
    f = wg.shape[2]
    row = pl.BlockSpec((tm, d), lambda i: (i, 0))

    def one_layer(rows, cols):
        return pl.BlockSpec((None, rows, cols), lambda i: (layer, 0, 0), pipeline_mode=pl.Buffered(1))

    return pl.pallas_call(
        functools.partial(_ffn_body, final_norm=final_norm),
        out_shape=jax.ShapeDtypeStruct((s, d), F32),
        grid=(s // tm,),
        in_specs=[row, _resident((1, d)), one_layer(d, f), one_layer(d, f),
                  one_layer(f, d), _resident((1, d))],
        out_specs=row,
        compiler_params=_params("arbitrary"),
        name="ffn",
    )(h, g, wg, wu, wd, g_final)


CAST_BLOCK_BYTES = 6 * 1024 * 1024


def _cast_body(x_ref, o_ref):
    o_ref[...] = x_ref[...].astype(BF16)


def _to_bf16(w):
    layers, rows, cols = w.shape
    fits = [t for t in range(16, rows + 1, 16) if rows % t == 0 and t * cols * 4 <= CAST_BLOCK_BYTES]
    tr = max(fits) if fits else rows
    blk = pl.BlockSpec((1, tr, cols), lambda i, j: (i, j, 0))
    return pl.pallas_call(
        _cast_body,
        out_shape=jax.ShapeDtypeStruct(w.shape, BF16),
        grid=(layers, rows // tr),
        in_specs=[blk],
        out_specs=blk,
        compiler_params=_params("arbitrary", "arbitrary"),
        name="to_bf16",
    )(w)


def _conv_body(h_ref, g_ref, w1_ref, b1_ref, wdw_ref, bdw_ref, lng_ref, lnb_ref,
               w2_ref, b2_ref, o_ref, ext_ref, cv_ref, wb_ref, *, tm, rb, lc):
    d = h_ref.shape[1]

    @pl.when(pl.program_id(0) == 0)
    def _():
        ext_ref[0:CONV_HALO, :] = jnp.zeros((CONV_HALO, d), F32)
        for j in range(CONV_WIDTH):
            wb_ref[j] = jnp.broadcast_to(wdw_ref[j:j + 1, :], (SUBLANES, d))

    x = h_ref[...]
    u = _rms(x, g_ref[...]).astype(BF16)
    y = _dot(u, w1_ref[...]) + b1_ref[...]
    ext_ref[CONV_HALO:CONV_HALO + tm, :] = y[:, :d] * jax.nn.sigmoid(y[:, d:])

    shift = CONV_HALO - (CONV_WIDTH - 1)

    def rows(r, carry):
        base = pl.multiple_of(r * rb, rb)
        for c0 in range(0, d, lc):
            lanes = slice(c0, c0 + lc)
            acc = jnp.broadcast_to(bdw_ref[:, lanes], (rb, lc))
            for res in range(SUBLANES):
                taps = [j for j in range(CONV_WIDTH) if (shift + j) % SUBLANES == res]
                nrows = rb + (SUBLANES if res else 0)
                part = None
                for j in taps:
                    off = pl.multiple_of(base + (shift + j - res), SUBLANES)
                    slabs = ext_ref[pl.ds(off, nrows), lanes].reshape(nrows // SUBLANES, SUBLANES, lc)
                    term = wb_ref[j, :, lanes] * slabs
                    part = term if part is None else part + term
                acc = acc + part.reshape(nrows, lc)[res:res + rb]
            cv_ref[pl.ds(base, rb), lanes] = acc
        return carry

    lax.fori_loop(0, tm // rb, rows, 0)
    ext_ref[0:CONV_HALO, :] = ext_ref[tm:tm + CONV_HALO, :]

    c = cv_ref[...]
    mu = jnp.mean(c, axis=-1, keepdims=True)
    cc = c - mu
    var = jnp.mean(cc * cc, axis=-1, keepdims=True)
    z = cc * lax.rsqrt(var + EPS) * lng_ref[...] + lnb_ref[...]
    z = (z * jax.nn.sigmoid(z)).astype(BF16)
    o_ref[...] = x + _dot(z, w2_ref[...]) + b2_ref[...]


def _conv_mixer(h, g, w1, b1, wdw, bdw, lng, lnb, w2, b2, *, tm=512, rb=64, lc=128):
    s, d = h.shape
    row = pl.BlockSpec((tm, d), lambda i: (i, 0))
    return pl.pallas_call(
        functools.partial(_conv_body, tm=tm, rb=rb, lc=lc),
        out_shape=jax.ShapeDtypeStruct((s, d), F32),
        grid=(s // tm,),
        in_specs=[row, _resident((1, d)), _resident((d, 2 * d)), _resident((1, 2 * d)),
                  _resident((CONV_WIDTH, d)), _resident((1, d)), _resident((1, d)),
                  _resident((1, d)), _resident((d, d)), _resident((1, d))],
        out_specs=row,
        scratch_shapes=[pltpu.VMEM((tm + CONV_HALO, d), F32), pltpu.VMEM((tm, d), F32),
                        pltpu.VMEM((CONV_WIDTH, SUBLANES, d), F32)],
        compiler_params=_params("arbitrary"),
        name="conv_mixer",
    )(h, g, w1, b1, wdw, bdw, lng, lnb, w2, b2)


def _gla_tables():
    c = GLA_CHUNK
    r = np.arange(c)[:, None]
    k = np.arange(c)[None, :]
    mats = [(k <= r), (k > r)]
    masks = [(k == r)]
    for lvl in range(1, GLA_LEVELS + 1):
        blk, half = 1 << lvl, 1 << (lvl - 1)
        mid = (r // blk) * blk + half - 1
        second = (r % blk) >= half
        mats.append(np.where(second, (k > mid) & (k <= r), (k > r) & (k <= mid)))
        masks.append(((r // blk) == (k // blk)) & second & ((k % blk) < half))
    return (np.concatenate(mats, 0).astype(np.float32),
            np.concatenate(masks, 0).astype(np.float32))


def _gla_body(h_ref, g_ref, wq_ref, wk_ref, wv_ref, wr_ref, wa1_ref, wa2_ref, ba_ref, br_ref,
              ng_ref, wo_ref, tab_ref, msk_ref, o_ref,
              q_s, k_s, v_s, la_s, r_s, og_s, st_s, *, tm):
    c = GLA_CHUNK
    dk = wq_ref.shape[1]
    dv = wv_ref.shape[1]
    hk = dk // GLA_HEADS
    hv = dv // GLA_HEADS

    @pl.when(pl.program_id(0) == 0)
    def _():
        st_s[...] = jnp.zeros(st_s.shape, F32)

    x = h_ref[...]
    u = _rms(x, g_ref[...]).astype(BF16)
    q_s[...] = _dot(u, wq_ref[...]) * hk ** -0.5
    k_s[...] = _dot(u, wk_ref[...])
    v_s[...] = _dot(u, wv_ref[...]).astype(BF16)
    lo = _dot(u, wa1_ref[...]).astype(BF16)
    la_s[...] = jax.nn.log_sigmoid(_dot(lo, wa2_ref[...]) + ba_ref[...]) / GLA_TAU
    rr = _dot(u, wr_ref[...]) + br_ref[...]
    r_s[...] = rr * jax.nn.sigmoid(rr)

    tab = tab_ref[...]

    def chunk(ci, carry):
        rows = pl.ds(pl.multiple_of(ci * c, c), c)
        la = la_s[rows, :]
        la1 = la.astype(BF16)
        rem = la - la1.astype(F32)
        la2 = rem.astype(BF16)
        la3 = (rem - la2.astype(F32)).astype(BF16)
        e = jnp.exp(_dot(tab, jnp.concatenate([la1, la2, la3], axis=0)))
        eb = e[0:c]
        ekl = e[c:2 * c]
        elast = e[c - 1:c]
        q = q_s[rows, :]
        k = k_s[rows, :]
        v = v_s[rows, :]
        for hd in range(GLA_HEADS):
            ks = slice(hd * hk, (hd + 1) * hk)
            vs = slice(hd * hv, (hd + 1) * hv)
            qh, kh, vh = q[:, ks], k[:, ks], v[:, vs]
            att = msk_ref[0:c, :] * _dot_nt(qh.astype(BF16), kh.astype(BF16))
            for lvl in range(1, GLA_LEVELS + 1):
                fl = e[(1 + lvl) * c:(2 + lvl) * c, ks]
                att = att + msk_ref[lvl * c:(lvl + 1) * c, :] * _dot_nt(
                    (qh * fl).astype(BF16), (kh * fl).astype(BF16))
            st = st_s[hd]
            o = _dot(att.astype(BF16), vh) + _dot_nt((qh * eb[:, ks]).astype(BF16), st.astype(BF16))
            st_s[hd] = elast[:, ks] * st + _dot_tn(vh, (kh * ekl[:, ks]).astype(BF16))
            on = _rms(o, ng_ref[...])
            og_s[rows, vs] = (on * r_s[rows, vs]).astype(BF16)
        return carry

    lax.fori_loop(0, tm // c, chunk, 0, unroll=True)
    o_ref[...] = x + _dot(og_s[...], wo_ref[...])


def _gla_mixer(h, g, wq, wk, wv, wr, wa1, wa2, ba, br, ng, wo, *, tm=256):
    s, d = h.shape
    dk, dv = wq.shape[1], wv.shape[1]
    tab, msk = _gla_tables()
    tab = jnp.asarray(np.concatenate([tab, tab, tab], axis=1), BF16)
    msk = jnp.asarray(msk, F32)
    row = pl.BlockSpec((tm, d), lambda i: (i, 0))
    return pl.pallas_call(
        functools.partial(_gla_body, tm=tm),
        out_shape=jax.ShapeDtypeStruct((s, d), F32),
        grid=(s // tm,),
        in_specs=[row, _resident((1, d)), _resident(wq.shape), _resident(wk.shape),
                  _resident(wv.shape), _resident(wr.shape), _resident(wa1.shape),
                  _resident(wa2.shape), _resident((1, dk)), _resident((1, dv)),
                  _resident((1, dv // GLA_HEADS)), _resident(wo.shape),
                  _resident(tab.shape), _resident(msk.shape)],
        out_specs=row,
        scratch_shapes=[pltpu.VMEM((tm, dk), F32), pltpu.VMEM((tm, dk), F32),
                        pltpu.VMEM((tm, dv), BF16), pltpu.VMEM((tm, dk), F32),
                        pltpu.VMEM((tm, dv), F32), pltpu.VMEM((tm, dv), BF16),
                        pltpu.VMEM((GLA_HEADS, dv // GLA_HEADS, dk // GLA_HEADS), F32)],
        compiler_params=_params("arbitrary"),
        name="gla_mixer",
    )(h, g, wq, wk, wv, wr, wa1, wa2, ba, br, ng, wo, tab, msk)


def _moba_proj_body(h_ref, g_ref, wq_ref, wk_ref, wvt_ref, q_ref, k_ref, vt_ref, sel_ref, km_s,
                    *, scale, nb):
    i = pl.program_id(0)
    blk, d = h_ref.shape
    dh = d // MOBA_HEADS

    @pl.when(i == 0)
    def _():
        km_s[...] = jnp.zeros(km_s.shape, F32)

    u = _rms(h_ref[...], g_ref[...]).astype(BF16)
    q = (_dot(u, wq_ref[...]) * scale).astype(BF16)
    q_ref[...] = q
    k = _dot(u, wk_ref[...])
    k_ref[...] = k.astype(BF16)
    vt = _dot_nt(wvt_ref[...], u).astype(BF16)
    for hd in range(MOBA_HEADS):
        base = hd * (dh + MOBA_ONES_ROWS)
        vt_ref[base:base + dh, :] = vt[hd * dh:(hd + 1) * dh]
        vt_ref[base + dh:base + dh + MOBA_ONES_ROWS, :] = jnp.ones((MOBA_ONES_ROWS, blk), BF16)

    blk_id = lax.broadcasted_iota(jnp.int32, (nb, blk), 0)
    kmb = km_s[...].astype(BF16)
    for hd in range(MOBA_HEADS):
        cols = slice(hd * dh, (hd + 1) * dh)
        gate = jnp.where(blk_id < i, _dot_nt(kmb[:, cols], q[:, cols]), -jnp.inf)
        sel = jnp.where(blk_id == i, 1.0, 0.0)
        for _ in range(MOBA_TOPK):
            best = jnp.max(gate, axis=0, keepdims=True)
            idx = jnp.min(jnp.where(gate == best, blk_id, nb), axis=0, keepdims=True)
            hit = blk_id == idx
            sel = jnp.where(hit & (best > -jnp.inf), 1.0, sel)
            gate = jnp.where(hit, -jnp.inf, gate)
        sel_ref[0, hd * nb:(hd + 1) * nb, :] = sel
    km_s[pl.ds(i, 1), :] = jnp.mean(k, axis=0, keepdims=True)


def _moba_proj(h, g, wq, wk, wvt):
    s, d = h.shape
    tm = MOBA_BLOCK
    nb = s // tm
    dv = d + MOBA_HEADS * MOBA_ONES_ROWS
    row = pl.BlockSpec((tm, d), lambda i: (i, 0))
    return pl.pallas_call(
        functools.partial(_moba_proj_body, scale=(d // MOBA_HEADS) ** -0.5, nb=nb),
        out_shape=[jax.ShapeDtypeStruct((s, d), BF16), jax.ShapeDtypeStruct((s, d), BF16),
                   jax.ShapeDtypeStruct((dv, s), BF16),
                   jax.ShapeDtypeStruct((nb, MOBA_HEADS * nb, tm), F32)],
        grid=(nb,),
        in_specs=[row, _resident((1, d)), _resident((d, d)), _resident((d, d)), _resident((d, d))],
        out_specs=[row, row, pl.BlockSpec((dv, tm), lambda i: (0, i)),
                   pl.BlockSpec((1, MOBA_HEADS * nb, tm), lambda i: (i, 0, 0))],
        scratch_shapes=[pltpu.VMEM((nb, d), F32)],
        compiler_params=_params("arbitrary"),
        name="moba_proj",
    )(h, g, wq, wk, wvt)


def _moba_attn_body(slope_ref, qa_ref, qb_ref, k_ref, vt_ref, sela_ref, selb_ref, oa_ref, ob_ref,
                    bias_ref, q_s, sel_s, m_s, acc_s, sa_ref, sb_ref, sc_ref, pa_ref, pb_ref,
                    *, nb, sub):
    blk = MOBA_BLOCK
    span = sub * blk
    hd = pl.program_id(0)
    p = pl.program_id(1)
    slope = slope_ref[hd]
    dh = qa_ref.shape[1]
    updates = nb // sub + 1
    first_b = (p + sub) // sub

    @pl.when(p == 0)
    def _():
        r = lax.broadcasted_iota(jnp.int32, (blk, blk), 0)
        c = lax.broadcasted_iota(jnp.int32, (blk, blk), 1)
        rel = (c - r).astype(F32)
        for g in range(sub):
            plain = slope * (rel - float(g * blk))
            bias_ref[g] = plain
            bias_ref[sub + g] = jnp.where(c >= r, plain, jnp.inf)

    q_s[0] = qa_ref[...]
    q_s[1] = qb_ref[...]
    sel_s[0] = sela_ref[0]
    sel_s[1] = selb_ref[0]
    m_s[...] = jnp.full(m_s.shape, MOBA_MIN_SCORE, F32)
    acc_s[...] = jnp.zeros(acc_s.shape, F32)

    def plan(u):
        is_b = u >= first_b
        slot = is_b.astype(jnp.int32)
        tile = jnp.where(is_b, nb - 1 - p, p)
        kb = jnp.where(is_b, u - first_b, u) * sub
        return slot, tile, kb

    def keys_of(kb):
        return pl.ds(pl.multiple_of(kb * blk, span), span)

    def scores(u, dst_ref):
        slot, _, kb = plan(u)
        dst_ref[...] = _dot_nt(k_ref[keys_of(kb), :], q_s[slot])

    def softmax(u, src_ref, dst_ref):
        slot, tile, kb = plan(u)
        m = m_s[slot]
        off = slope * ((tile - kb) * blk).astype(F32)
        chosen = [sel_s[slot, pl.ds(kb + g, 1), :] > 0.0 for g in range(sub)]
        top = None
        for g in range(sub):
            rows = slice(g * blk, (g + 1) * blk)
            biased = src_ref[rows, :] - bias_ref[jnp.where(kb + g == tile, sub + g, g)]
            src_ref[rows, :] = biased
            mg = jnp.where(chosen[g], jnp.max(biased, axis=0, keepdims=True), -jnp.inf)
            top = mg if top is None else jnp.maximum(top, mg)
        m_new = jnp.maximum(m, top - off)
        m_s[slot] = m_new
        for g in range(sub):
            shift = jnp.where(chosen[g], m_new + off, jnp.inf)
            dst_ref[g * blk:(g + 1) * blk, :] = jnp.exp(src_ref[g * blk:(g + 1) * blk, :] - shift).astype(BF16)
        return jnp.exp(m - m_new)

    def add_values(u, p_ref):
        slot, _, kb = plan(u)
        acc_s[slot] = acc_s[slot] + _dot(vt_ref[:, keys_of(kb)], p_ref[...])

    stage_s = (sa_ref, sb_ref, sc_ref)
    stage_p = (pa_ref, pb_ref)
    scores(0, sa_ref)
    for u in range(updates):
        if u + 1 < updates:
            scores(u + 1, stage_s[(u + 1) % 3])
        alpha = softmax(u, stage_s[u % 3], stage_p[u % 2])
        if u > 0:
            add_values(u - 1, stage_p[(u - 1) % 2])
        slot = plan(u)[0]
        acc_s[slot] = alpha * acc_s[slot]
    add_values(updates - 1, stage_p[(updates - 1) % 2])

    for slot, o_ref in ((0, oa_ref), (1, ob_ref)):
        acc = acc_s[slot]
        o_ref[...] = (acc[0:dh] / acc[dh:dh + 1]).T.astype(BF16)


def _moba_attn(q, k, vt, sel, slopes, *, sub=4):
    s, d = q.shape
    dh = d // MOBA_HEADS
    blk = MOBA_BLOCK
    nb = s // blk
    half = nb // 2
    sub = min(sub, half)
    assert nb % (2 * sub) == 0
    scores = pltpu.VMEM((sub * blk, blk), F32)
    probs = pltpu.VMEM((sub * blk, blk), BF16)
    tile_a = lambda h, p, _: (p, h)
    tile_b = lambda h, p, _: (nb - 1 - p, h)
    return pl.pallas_call(
        functools.partial(_moba_attn_body, nb=nb, sub=sub),
        out_shape=[jax.ShapeDtypeStruct((s // 2, d), BF16)] * 2,
        grid_spec=pltpu.PrefetchScalarGridSpec(
            num_scalar_prefetch=1,
            grid=(MOBA_HEADS, half),
            in_specs=[pl.BlockSpec((blk, dh), tile_a),
                      pl.BlockSpec((blk, dh), tile_b),
                      pl.BlockSpec((s, dh), lambda h, p, _: (0, h)),
                      pl.BlockSpec((dh + MOBA_ONES_ROWS, s), lambda h, p, _: (h, 0)),
                      pl.BlockSpec((1, nb, blk), lambda h, p, _: (p, h, 0)),
                      pl.BlockSpec((1, nb, blk), lambda h, p, _: (nb - 1 - p, h, 0))],
            out_specs=[pl.BlockSpec((blk, dh), tile_a),
                       pl.BlockSpec((blk, dh), lambda h, p, _: (half - 1 - p, h))],
            scratch_shapes=[pltpu.VMEM((2 * sub, blk, blk), F32),
                            pltpu.VMEM((2, blk, dh), BF16),
                            pltpu.VMEM((2, nb, blk), F32),
                            pltpu.VMEM((2, 1, blk), F32),
                            pltpu.VMEM((2, dh + MOBA_ONES_ROWS, blk), F32),
                            scores, scores, scores, probs, probs]),
        compiler_params=_params("arbitrary", "arbitrary"),
        name="moba_attn",
    )(slopes, q, q, k, vt, sel, sel)


def _proj_residual_body(h_ref, a_ref, w_ref, o_ref):
    o_ref[...] = h_ref[...] + _dot(a_ref[...], w_ref[...])


def _proj_residual(h, a, w, *, tm=512):
    s, d = h.shape
    row = pl.BlockSpec((tm, d), lambda i: (i, 0))
    return pl.pallas_call(
        _proj_residual_body,
        out_shape=jax.ShapeDtypeStruct((s, d), F32),
        grid=(s // tm,),
        in_specs=[row, pl.BlockSpec((tm, a.shape[1]), lambda i: (i, 0)), _resident(w.shape)],
        out_specs=row,
        compiler_params=_params("arbitrary"),
        name="proj_residual",
    )(h, a, w)


def _moba_mixer(h, g, wq, wk, wvt, wo):
    s, d = h.shape
    q, k, vt, sel = _moba_proj(h, g, wq, wk, wvt)
    heads = jnp.arange(1, MOBA_HEADS + 1, dtype=F32)
    slopes = 2.0 ** (-8.0 * heads / MOBA_HEADS)
    o_lo, o_hi = _moba_attn(q, k, vt, sel, slopes)
    return _proj_residual(h, jnp.concatenate([o_lo, o_hi], axis=0), wo)


def kernel(x, g_ffn1, g_mix, g_ffn2, g_final, w1_gate, w1_up, w1_down, w2_gate, w2_up, w2_down, cv_w_pw1, cv_b_pw1, cv_w_dw, cv_b_dw, cv_ln_g, cv_ln_b, cv_w_pw2, cv_b_pw2, gla_w_q, gla_w_k, gla_w_v, gla_w_a1, gla_w_a2, gla_b_a, gla_w_r, gla_b_r, gla_norm_g, gla_w_o, mb_w_q, mb_w_k, mb_w_v, mb_w_o):
    batch, seq, d = x.shape
    depth = g_ffn1.shape[0]
    assert batch == 1
    bf = lambda w: w.astype(BF16)
    vec = lambda b: b.reshape(1, -1)
    h = x.reshape(seq, d)
    gf = vec(g_final)
    ffn1 = [_to_bf16(w) for w in (w1_gate, w1_up, w1_down)]
    ffn2 = [_to_bf16(w) for w in (w2_gate, w2_up, w2_down)]
    cv_pw1, cv_pw2 = _to_bf16(cv_w_pw1), _to_bf16(cv_w_pw2)
    gla_q, gla_k, gla_v, gla_r, gla_o = [
        _to_bf16(w) for w in (gla_w_q, gla_w_k, gla_w_v, gla_w_r, gla_w_o)]
    mb_q, mb_k, mb_o = [_to_bf16(w) for w in (mb_w_q, mb_w_k, mb_w_o)]
    for i in range(depth):
        h = _ffn(h, vec(g_ffn1[i]), *ffn1, gf, layer=i, final_norm=False)
        kind, j = i % 3, i // 3
        gm = vec(g_mix[i])
        if kind == 0:
            h = _conv_mixer(h, gm, cv_pw1[j], vec(cv_b_pw1[j]), cv_w_dw[j], vec(cv_b_dw[j]),
                            vec(cv_ln_g[j]), vec(cv_ln_b[j]), cv_pw2[j], vec(cv_b_pw2[j]))
        elif kind == 1:
            rank = gla_w_a1.shape[2]
            wa1 = jnp.pad(gla_w_a1[j], ((0, 0), (0, GLA_RANK_PAD - rank)))
            wa2 = jnp.pad(gla_w_a2[j], ((0, GLA_RANK_PAD - rank), (0, 0)))
            h = _gla_mixer(h, gm, gla_q[j], gla_k[j], gla_v[j], gla_r[j],
                           bf(wa1), bf(wa2), vec(gla_b_a[j]), vec(gla_b_r[j]),
                           vec(gla_norm_g[j]), gla_o[j])
        else:
            h = _moba_mixer(h, gm, mb_q[j], mb_k[j], bf(mb_w_v[j].T), mb_o[j])
        h = _ffn(h, vec(g_ffn2[i]), *ffn2, gf, layer=i, final_norm=(i == depth - 1))
    return h.reshape(batch, seq, d)
```

```python
import functools

import jax
import jax.numpy as jnp
import numpy as np
from jax import lax
from jax.experimental import pallas as pl
from jax.experimental.pallas import tpu as pltpu

EPS = 1e-6
CONV_WIDTH = 31
GLA_HEADS = 4
GLA_TAU = 16.0
GLA_CHUNK = 64
GLA_LEVELS = 6
GLA_RANK_PAD = 128
MOBA_HEADS = 8
MOBA_BLOCK = 256
MOBA_TOPK = 3
MOBA_ONES_ROWS = 16
MOBA_MIN_SCORE = float(np.finfo(np.float32).min) / 2

V7X_VMEM_LIMIT_BYTES = 56 * 1024 * 1024
SUBLANES = 8
CONV_HALO = 32

BF16 = jnp.bfloat16
F32 = jnp.float32


def _params(*semantics):
    return pltpu.CompilerParams(dimension_semantics=semantics,
                                vmem_limit_bytes=V7X_VMEM_LIMIT_BYTES)


def _resident(shape):
    return pl.BlockSpec(shape, lambda *_: (0,) * len(shape), pipeline_mode=pl.Buffered(1))


def _rms(x, g):
    return x * lax.rsqrt(jnp.mean(x * x, axis=-1, keepdims=True) + EPS) * g


def _dot(a, b):
    return jnp.dot(a, b, preferred_element_type=F32)


def _dot_nt(a, b):
    return lax.dot_general(a, b, (((1,), (1,)), ((), ())), preferred_element_type=F32)


def _dot_tn(a, b):
    return lax.dot_general(a, b, (((0,), (0,)), ((), ())), preferred_element_type=F32)


def _ffn_body(h_ref, g_ref, wg_ref, wu_ref, wd_ref, gf_ref, o_ref, *, final_norm):
    x = h_ref[...]
    xn = _rms(x, g_ref[...]).astype(BF16)
    gate = _dot(xn, wg_ref[...])
    up = _dot(xn, wu_ref[...])
    a = (gate * jax.nn.sigmoid(gate) * up).astype(BF16)
    y = x + 0.5 * _dot(a, wd_ref[...])
    if final_norm:
        y = _rms(y, gf_ref[...])
    o_ref[...] = y


def _ffn(h, g, wg, wu, wd, g_final, *, layer, final_norm, tm=512):
    s, d = h.shape
    f = wg.shape[2]
    row = pl.BlockSpec((tm, d), lambda i: (i, 0))

    def one_layer(rows, cols):
        return pl.BlockSpec((None, rows, cols), lambda i: (layer, 0, 0), pipeline_mode=pl.Buffered(1))

    return pl.pallas_call(
        functools.partial(_ffn_body, final_norm=final_norm),
        out_shape=jax.ShapeDtypeStruct((s, d), F32),
        grid=(s // tm,),
        in_specs=[row, _resident((1, d)), one_layer(d, f), one_layer(d, f),
                  one_layer(f, d), _resident((1, d))],
        out_specs=row,
        compiler_params=_params("arbitrary"),
        name="ffn",
    )(h, g, wg, wu, wd, g_final)


CAST_BLOCK_BYTES = 6 * 1024 * 1024


def _cast_body(x_ref, o_ref):
    o_ref[...] = x_ref[...].astype(BF16)


def _to_bf16(w):
    layers, rows, cols = w.shape
    fits = [t for t in range(16, rows + 1, 16) if rows % t == 0 and t * cols * 4 <= CAST_BLOCK_BYTES]
    tr = max(fits) if fits else rows
    blk = pl.BlockSpec((1, tr, cols), lambda i, j: (i, j, 0))
    return pl.pallas_call(
        _cast_body,
        out_shape=jax.ShapeDtypeStruct(w.shape, BF16),
        grid=(layers, rows // tr),
        in_specs=[blk],
        out_specs=blk,
        compiler_params=_params("arbitrary", "arbitrary"),
        name="to_bf16",
    )(w)


def _conv_body(h_ref, g_ref, w1_ref, b1_ref, wdw_ref, bdw_ref, lng_ref, lnb_ref,
               w2_ref, b2_ref, o_ref, ext_ref, cv_ref, wb_ref, *, tm, rb, lc):
    d = h_ref.shape[1]

    @pl.when(pl.program_id(0) == 0)
    def _():
        ext_ref[0:CONV_HALO, :] = jnp.zeros((CONV_HALO, d), F32)
        for j in range(CONV_WIDTH):
            wb_ref[j] = jnp.broadcast_to(wdw_ref[j:j + 1, :], (SUBLANES, d))

    x = h_ref[...]
    u = _rms(x, g_ref[...]).astype(BF16)
    y = _dot(u, w1_ref[...]) + b1_ref[...]
    ext_ref[CONV_HALO:CONV_HALO + tm, :] = y[:, :d] * jax.nn.sigmoid(y[:, d:])

    shift = CONV_HALO - (CONV_WIDTH - 1)

    def rows(r, carry):
        base = pl.multiple_of(r * rb, rb)
        for c0 in range(0, d, lc):
            lanes = slice(c0, c0 + lc)
            acc = jnp.broadcast_to(bdw_ref[:, lanes], (rb, lc))
            for res in range(SUBLANES):
                taps = [j for j in range(CONV_WIDTH) if (shift + j) % SUBLANES == res]
                nrows = rb + (SUBLANES if res else 0)
                part = None
                for j in taps:
                    off = pl.multiple_of(base + (shift + j - res), SUBLANES)
                    slabs = ext_ref[pl.ds(off, nrows), lanes].reshape(nrows // SUBLANES, SUBLANES, lc)
                    term = wb_ref[j, :, lanes] * slabs
                    part = term if part is None else part + term
                acc = acc + part.reshape(nrows, lc)[res:res + rb]
            cv_ref[pl.ds(base, rb), lanes] = acc
        return carry

    lax.fori_loop(0, tm // rb, rows, 0)
    ext_ref[0:CONV_HALO, :] = ext_ref[tm:tm + CONV_HALO, :]

    c = cv_ref[...]
    mu = jnp.mean(c, axis=-1, keepdims=True)
    cc = c - mu
    var = jnp.mean(cc * cc, axis=-1, keepdims=True)
    z = cc * lax.rsqrt(var + EPS) * lng_ref[...] + lnb_ref[...]
    z = (z * jax.nn.sigmoid(z)).astype(BF16)
    o_ref[...] = x + _dot(z, w2_ref[...]) + b2_ref[...]


def _conv_mixer(h, g, w1, b1, wdw, bdw, lng, lnb, w2, b2, *, tm=512, rb=64, lc=128):
    s, d = h.shape
    row = pl.BlockSpec((tm, d), lambda i: (i, 0))
    return pl.pallas_call(
        functools.partial(_conv_body, tm=tm, rb=rb, lc=lc),
        out_shape=jax.ShapeDtypeStruct((s, d), F32),
        grid=(s // tm,),
        in_specs=[row, _resident((1, d)), _resident((d, 2 * d)), _resident((1, 2 * d)),
                  _resident((CONV_WIDTH, d)), _resident((1, d)), _resident((1, d)),
                  _resident((1, d)), _resident((d, d)), _resident((1, d))],
        out_specs=row,
        scratch_shapes=[pltpu.VMEM((tm + CONV_HALO, d), F32), pltpu.VMEM((tm, d), F32),
                        pltpu.VMEM((CONV_WIDTH, SUBLANES, d), F32)],
        compiler_params=_params("arbitrary"),
        name="conv_mixer",
    )(h, g, w1, b1, wdw, bdw, lng, lnb, w2, b2)


def _gla_tables():
    c = GLA_CHUNK
    r = np.arange(c)[:, None]
    k = np.arange(c)[None, :]
    mats = [(k <= r), (k > r)]
    masks = [(k == r)]
    for lvl in range(1, GLA_LEVELS + 1):
        blk, half = 1 << lvl, 1 << (lvl - 1)
        mid = (r // blk) * blk + half - 1
        second = (r % blk) >= half
        mats.append(np.where(second, (k > mid) & (k <= r), (k > r) & (k <= mid)))
        masks.append(((r // blk) == (k // blk)) & second & ((k % blk) < half))
    return (np.concatenate(mats, 0).astype(np.float32),
            np.concatenate(masks, 0).astype(np.float32))


def _gla_body(h_ref, g_ref, wq_ref, wk_ref, wv_ref, wr_ref, wa1_ref, wa2_ref, ba_ref, br_ref,
              ng_ref, wo_ref, tab_ref, msk_ref, o_ref,
              q_s, k_s, v_s, la_s, r_s, og_s, st_s, *, tm):
    c = GLA_CHUNK
    dk = wq_ref.shape[1]
    dv = wv_ref.shape[1]
    hk = dk // GLA_HEADS
    hv = dv // GLA_HEADS

    @pl.when(pl.program_id(0) == 0)
    def _():
        st_s[...] = jnp.zeros(st_s.shape, F32)

    x = h_ref[...]
    u = _rms(x, g_ref[...]).astype(BF16)
    q_s[...] = _dot(u, wq_ref[...]) * hk ** -0.5
    k_s[...] = _dot(u, wk_ref[...])
    v_s[...] = _dot(u, wv_ref[...]).astype(BF16)
    lo = _dot(u, wa1_ref[...]).astype(BF16)
    la_s[...] = jax.nn.log_sigmoid(_dot(lo, wa2_ref[...]) + ba_ref[...]) / GLA_TAU
    rr = _dot(u, wr_ref[...]) + br_ref[...]
    r_s[...] = rr * jax.nn.sigmoid(rr)

    tab = tab_ref[...]

    def chunk(ci, carry):
        rows = pl.ds(pl.multiple_of(ci * c, c), c)
        la = la_s[rows, :]
        la1 = la.astype(BF16)
        rem = la - la1.astype(F32)
        la2 = rem.astype(BF16)
        la3 = (rem - la2.astype(F32)).astype(BF16)
        e = jnp.exp(_dot(tab, jnp.concatenate([la1, la2, la3], axis=0)))
        eb = e[0:c]
        ekl = e[c:2 * c]
        elast = e[c - 1:c]
        q = q_s[rows, :]
        k = k_s[rows, :]
        v = v_s[rows, :]
        for hd in range(GLA_HEADS):
            ks = slice(hd * hk, (hd + 1) * hk)
            vs = slice(hd * hv, (hd + 1) * hv)
            qh, kh, vh = q[:, ks], k[:, ks], v[:, vs]
            att = msk_ref[0:c, :] * _dot_nt(qh.astype(BF16), kh.astype(BF16))
            for lvl in range(1, GLA_LEVELS + 1):
                fl = e[(1 + lvl) * c:(2 + lvl) * c, ks]
                att = att + msk_ref[lvl * c:(lvl + 1) * c, :] * _dot_nt(
                    (qh * fl).astype(BF16), (kh * fl).astype(BF16))
            st = st_s[hd]
            o = _dot(att.astype(BF16), vh) + _dot_nt((qh * eb[:, ks]).astype(BF16), st.astype(BF16))
            st_s[hd] = elast[:, ks] * st + _dot_tn(vh, (kh * ekl[:, ks]).astype(BF16))
            on = _rms(o, ng_ref[...])
            og_s[rows, vs] = (on * r_s[rows, vs]).astype(BF16)
        return carry

    lax.fori_loop(0, tm // c, chunk, 0, unroll=True)
    o_ref[...] = x + _dot(og_s[...], wo_ref[...])


def _gla_mixer(h, g, wq, wk, wv, wr, wa1, wa2, ba, br, ng, wo, *, tm=256):
    s, d = h.shape
    dk, dv = wq.shape[1], wv.shape[1]
    tab, msk = _gla_tables()
    tab = jnp.asarray(np.concatenate([tab, tab, tab], axis=1), BF16)
    msk = jnp.asarray(msk, F32)
    row = pl.BlockSpec((tm, d), lambda i: (i, 0))
    return pl.pallas_call(
        functools.partial(_gla_body, tm=tm),
        out_shape=jax.ShapeDtypeStruct((s, d), F32),
        grid=(s // tm,),
        in_specs=[row, _resident((1, d)), _resident(wq.shape), _resident(wk.shape),
                  _resident(wv.shape), _resident(wr.shape), _resident(wa1.shape),
                  _resident(wa2.shape), _resident((1, dk)), _resident((1, dv)),
                  _resident((1, dv // GLA_HEADS)), _resident(wo.shape),
                  _resident(tab.shape), _resident(msk.shape)],
        out_specs=row,
        scratch_shapes=[pltpu.VMEM((tm, dk), F32), pltpu.VMEM((tm, dk), F32),
                        pltpu.VMEM((tm, dv), BF16), pltpu.VMEM((tm, dk), F32),
                        pltpu.VMEM((tm, dv), F32), pltpu.VMEM((tm, dv), BF16),
                        pltpu.VMEM((GLA_HEADS, dv // GLA_HEADS, dk // GLA_HEADS), F32)],
        compiler_params=_params("arbitrary"),
        name="gla_mixer",
    )(h, g, wq, wk, wv, wr, wa1, wa2, ba, br, ng, wo, tab, msk)


def _moba_proj_body(h_ref, g_ref, wq_ref, wk_ref, wvt_ref, q_ref, k_ref, vt_ref, sel_ref, km_s,
                    *, scale, nb):
    i = pl.program_id(0)
    blk, d = h_ref.shape
    dh = d // MOBA_HEADS

    @pl.when(i == 0)
    def _():
        km_s[...] = jnp.zeros(km_s.shape, F32)

    u = _rms(h_ref[...], g_ref[...]).astype(BF16)
    q = (_dot(u, wq_ref[...]) * scale).astype(BF16)
    q_ref[...] = q
    k = _dot(u, wk_ref[...])
    k_ref[...] = k.astype(BF16)
    vt = _dot_nt(wvt_ref[...], u).astype(BF16)
    for hd in range(MOBA_HEADS):
        base = hd * (dh + MOBA_ONES_ROWS)
        vt_ref[base:base + dh, :] = vt[hd * dh:(hd + 1) * dh]
        vt_ref[base + dh:base + dh + MOBA_ONES_ROWS, :] = jnp.ones((MOBA_ONES_ROWS, blk), BF16)

    blk_id = lax.broadcasted_iota(jnp.int32, (nb, blk), 0)
    kmb = km_s[...].astype(BF16)
    for hd in range(MOBA_HEADS):
        cols = slice(hd * dh, (hd + 1) * dh)
        gate = jnp.where(blk_id < i, _dot_nt(kmb[:, cols], q[:, cols]), -jnp.inf)
        sel = jnp.where(blk_id == i, 1.0, 0.0)
        for _ in range(MOBA_TOPK):
            best = jnp.max(gate, axis=0, keepdims=True)
            idx = jnp.min(jnp.where(gate == best, blk_id, nb), axis=0, keepdims=True)
            hit = blk_id == idx
            sel = jnp.where(hit & (best > -jnp.inf), 1.0, sel)
            gate = jnp.where(hit, -jnp.inf, gate)
        sel_ref[0, hd * nb:(hd + 1) * nb, :] = sel
    km_s[pl.ds(i, 1), :] = jnp.mean(k, axis=0, keepdims=True)


def _moba_proj(h, g, wq, wk, wvt):
    s, d = h.shape
    tm = MOBA_BLOCK
    nb = s // tm
    dv = d + MOBA_HEADS * MOBA_ONES_ROWS
    row = pl.BlockSpec((tm, d), lambda i: (i, 0))
    return pl.pallas_call(
        functools.partial(_moba_proj_body, scale=(d // MOBA_HEADS) ** -0.5, nb=nb),
        out_shape=[jax.ShapeDtypeStruct((s, d), BF16), jax.ShapeDtypeStruct((s, d), BF16),
                   jax.ShapeDtypeStruct((dv, s), BF16),
                   jax.ShapeDtypeStruct((nb, MOBA_HEADS * nb, tm), F32)],
        grid=(nb,),
        in_specs=[row, _resident((1, d)), _resident((d, d)), _resident((d, d)), _resident((d, d))],
        out_specs=[row, row, pl.BlockSpec((dv, tm), lambda i: (0, i)),
                   pl.BlockSpec((1, MOBA_HEADS * nb, tm), lambda i: (i, 0, 0))],
        scratch_shapes=[pltpu.VMEM((nb, d), F32)],
        compiler_params=_params("arbitrary"),
        name="moba_proj",
    )(h, g, wq, wk, wvt)


def _moba_attn_body(slope_ref, qa_ref, qb_ref, k_ref, vt_ref, sela_ref, selb_ref, oa_ref, ob_ref,
                    bias_ref, q_s, sel_s, m_s, acc_s, sa_ref, sb_ref, sc_ref, pa_ref, pb_ref,
                    *, nb, sub):
    blk = MOBA_BLOCK
    span = sub * blk
    hd = pl.program_id(0)
    p = pl.program_id(1)
    slope = slope_ref[hd]
    dh = qa_ref.shape[1]
    updates = nb // sub + 1
    first_b = (p + sub) // sub

    @pl.when(p == 0)
    def _():
        r = lax.broadcasted_iota(jnp.int32, (blk, blk), 0)
        c = lax.broadcasted_iota(jnp.int32, (blk, blk), 1)
        rel = (c - r).astype(F32)
        for g in range(sub):
            plain = slope * (rel - float(g * blk))
            bias_ref[g] = plain
            bias_ref[sub + g] = jnp.where(c >= r, plain, jnp.inf)

    q_s[0] = qa_ref[...]
    q_s[1] = qb_ref[...]
    sel_s[0] = sela_ref[0]
    sel_s[1] = selb_ref[0]
    m_s[...] = jnp.full(m_s.shape, MOBA_MIN_SCORE, F32)
    acc_s[...] = jnp.zeros(acc_s.shape, F32)

    def plan(u):
        is_b = u >= first_b
        slot = is_b.astype(jnp.int32)
        tile = jnp.where(is_b, nb - 1 - p, p)
        kb = jnp.where(is_b, u - first_b, u) * sub
        return slot, tile, kb

    def keys_of(kb):
        return pl.ds(pl.multiple_of(kb * blk, span), span)

    def scores(u, dst_ref):
        slot, _, kb = plan(u)
        dst_ref[...] = _dot_nt(k_ref[keys_of(kb), :], q_s[slot])

    def softmax(u, src_ref, dst_ref):
        slot, tile, kb = plan(u)
        m = m_s[slot]
        off = slope * ((tile - kb) * blk).astype(F32)
        chosen = [sel_s[slot, pl.ds(kb + g, 1), :] > 0.0 for g in range(sub)]
        top = None
        for g in range(sub):
            rows = slice(g * blk, (g + 1) * blk)
            biased = src_ref[rows, :] - bias_ref[jnp.where(kb + g == tile, sub + g, g)]
            src_ref[rows, :] = biased
            mg = jnp.where(chosen[g], jnp.max(biased, axis=0, keepdims=True), -jnp.inf)
            top = mg if top is None else jnp.maximum(top, mg)
        m_new = jnp.maximum(m, top - off)
        m_s[slot] = m_new
        zero = jnp.minimum(p, 0) * blk
        for g in range(sub):
            shift = jnp.where(chosen[g], m_new + off, jnp.inf)
            rows = pl.ds(pl.multiple_of(zero + g * blk, blk), blk)
            dst_ref[g * blk:(g + 1) * blk, :] = jnp.exp(src_ref[rows, :] - shift).astype(BF16)
        return jnp.exp(m - m_new)

    def add_values(u, p_ref):
        slot, _, kb = plan(u)
        acc_s[slot] = acc_s[slot] + _dot(vt_ref[:, keys_of(kb)], p_ref[...])

    stage_s = (sa_ref, sb_ref, sc_ref)
    stage_p = (pa_ref, pb_ref)
    scores(0, sa_ref)
    for u in range(updates):
        if u + 1 < updates:
            scores(u + 1, stage_s[(u + 1) % 3])
        alpha = softmax(u, stage_s[u % 3], stage_p[u % 2])
        if u > 0:
            add_values(u - 1, stage_p[(u - 1) % 2])
        slot = plan(u)[0]
        acc_s[slot] = alpha * acc_s[slot]
    add_values(updates - 1, stage_p[(updates - 1) % 2])

    for slot, o_ref in ((0, oa_ref), (1, ob_ref)):
        acc = acc_s[slot]
        o_ref[...] = (acc[0:dh] / acc[dh:dh + 1]).T.astype(BF16)


def _moba_attn(q, k, vt, sel, slopes, *, sub=4):
    s, d = q.shape
    dh = d // MOBA_HEADS
    blk = MOBA_BLOCK
    nb = s // blk
    half = nb // 2
    sub = min(sub, half)
    assert nb % (2 * sub) == 0
    scores = pltpu.VMEM((sub * blk, blk), F32)
    probs = pltpu.VMEM((sub * blk, blk), BF16)
    tile_a = lambda h, p, _: (p, h)
    tile_b = lambda h, p, _: (nb - 1 - p, h)
    return pl.pallas_call(
        functools.partial(_moba_attn_body, nb=nb, sub=sub),
        out_shape=[jax.ShapeDtypeStruct((s // 2, d), BF16)] * 2,
        grid_spec=pltpu.PrefetchScalarGridSpec(
            num_scalar_prefetch=1,
            grid=(MOBA_HEADS, half),
            in_specs=[pl.BlockSpec((blk, dh), tile_a),
                      pl.BlockSpec((blk, dh), tile_b),
                      pl.BlockSpec((s, dh), lambda h, p, _: (0, h)),
                      pl.BlockSpec((dh + MOBA_ONES_ROWS, s), lambda h, p, _: (h, 0)),
                      pl.BlockSpec((1, nb, blk), lambda h, p, _: (p, h, 0)),
                      pl.BlockSpec((1, nb, blk), lambda h, p, _: (nb - 1 - p, h, 0))],
            out_specs=[pl.BlockSpec((blk, dh), tile_a),
                       pl.BlockSpec((blk, dh), lambda h, p, _: (half - 1 - p, h))],
            scratch_shapes=[pltpu.VMEM((2 * sub, blk, blk), F32),
                            pltpu.VMEM((2, blk, dh), BF16),
                            pltpu.VMEM((2, nb, blk), F32),
                            pltpu.VMEM((2, 1, blk), F32),
                            pltpu.VMEM((2, dh + MOBA_ONES_ROWS, blk), F32),
                            scores, scores, scores, probs, probs]),
        compiler_params=_params("arbitrary", "arbitrary"),
        name="moba_attn",
    )(slopes, q, q, k, vt, sel, sel)


def _proj_residual_body(h_ref, lo_ref, hi_ref, w_ref, o_ref, *, half_steps):
    a = jnp.where(pl.program_id(0) < half_steps, lo_ref[...], hi_ref[...])
    o_ref[...] = h_ref[...] + _dot(a, w_ref[...])


def _proj_residual(h, a_lo, a_hi, w, *, tm=512):
    s, d = h.shape
    half_steps = a_lo.shape[0] // tm
    row = pl.BlockSpec((tm, d), lambda i: (i, 0))
    return pl.pallas_call(
        functools.partial(_proj_residual_body, half_steps=half_steps),
        out_shape=jax.ShapeDtypeStruct((s, d), F32),
        grid=(s // tm,),
        in_specs=[row,
                  pl.BlockSpec((tm, a_lo.shape[1]), lambda i: (jnp.minimum(i, half_steps - 1), 0)),
                  pl.BlockSpec((tm, a_hi.shape[1]), lambda i: (jnp.maximum(i - half_steps, 0), 0)),
                  _resident(w.shape)],
        out_specs=row,
        compiler_params=_params("arbitrary"),
        name="proj_residual",
    )(h, a_lo, a_hi, w)


def _moba_mixer(h, g, wq, wk, wvt, wo):
    s, d = h.shape
    q, k, vt, sel = _moba_proj(h, g, wq, wk, wvt)
    heads = jnp.arange(1, MOBA_HEADS + 1, dtype=F32)
    slopes = 2.0 ** (-8.0 * heads / MOBA_HEADS)
    o_lo, o_hi = _moba_attn(q, k, vt, sel, slopes)
    return _proj_residual(h, o_lo, o_hi, wo)


def kernel(x, g_ffn1, g_mix, g_ffn2, g_final, w1_gate, w1_up, w1_down, w2_gate, w2_up, w2_down, cv_w_pw1, cv_b_pw1, cv_w_dw, cv_b_dw, cv_ln_g, cv_ln_b, cv_w_pw2, cv_b_pw2, gla_w_q, gla_w_k, gla_w_v, gla_w_a1, gla_w_a2, gla_b_a, gla_w_r, gla_b_r, gla_norm_g, gla_w_o, mb_w_q, mb_w_k, mb_w_v, mb_w_o):
    batch, seq, d = x.shape
    depth = g_ffn1.shape[0]
    assert batch == 1
    bf = lambda w: w.astype(BF16)
    vec = lambda b: b.reshape(1, -1)
    h = x.reshape(seq, d)
    gf = vec(g_final)
    ffn1 = [_to_bf16(w) for w in (w1_gate, w1_up, w1_down)]
    ffn2 = [_to_bf16(w) for w in (w2_gate, w2_up, w2_down)]
    cv_pw1, cv_pw2 = _to_bf16(cv_w_pw1), _to_bf16(cv_w_pw2)
    gla_q, gla_k, gla_v, gla_r, gla_o = [
        _to_bf16(w) for w in (gla_w_q, gla_w_k, gla_w_v, gla_w_r, gla_w_o)]
    mb_q, mb_k, mb_o = [_to_bf16(w) for w in (mb_w_q, mb_w_k, mb_w_o)]
    for i in range(depth):
        h = _ffn(h, vec(g_ffn1[i]), *ffn1, gf, layer=i, final_norm=False)
        kind, j = i % 3, i // 3
        gm = vec(g_mix[i])
        if kind == 0:
            h = _conv_mixer(h, gm, cv_pw1[j], vec(cv_b_pw1[j]), cv_w_dw[j], vec(cv_b_dw[j]),
                            vec(cv_ln_g[j]), vec(cv_ln_b[j]), cv_pw2[j], vec(cv_b_pw2[j]))
        elif kind == 1:
            rank = gla_w_a1.shape[2]
            wa1 = jnp.pad(gla_w_a1[j], ((0, 0), (0, GLA_RANK_PAD - rank)))
            wa2 = jnp.pad(gla_w_a2[j], ((0, GLA_RANK_PAD - rank), (0, 0)))
            h = _gla_mixer(h, gm, gla_q[j], gla_k[j], gla_v[j], gla_r[j],
                           bf(wa1), bf(wa2), vec(gla_b_a[j]), vec(gla_b_r[j]),
                           vec(gla_norm_g[j]), gla_o[j])
        else:
            h = _moba_mixer(h, gm, mb_q[j], mb_k[j], bf(mb_w_v[j].T), mb_o[j])
        h = _ffn(h, vec(g_ffn2[i]), *ffn2, gf, layer=i, final_norm=(i == depth - 1))
    return h.reshape(batch, seq, d)
```

```python
import functools

import jax
import jax.numpy as jnp
import numpy as np
from jax import lax
from jax.experimental import pallas as pl
from jax.experimental.pallas import tpu as pltpu

EPS = 1e-6
CONV_WIDTH = 31
GLA_HEADS = 4
GLA_TAU = 16.0
GLA_CHUNK = 64
GLA_LEVELS = 6
GLA_RANK_PAD = 128
MOBA_HEADS = 8
MOBA_BLOCK = 256
MOBA_TOPK = 3
MOBA_ONES_ROWS = 16
MOBA_MIN_SCORE = float(np.finfo(np.float32).min) / 2

V7X_VMEM_LIMIT_BYTES = 56 * 1024 * 1024
SUBLANES = 8
CONV_HALO = 32

BF16 = jnp.bfloat16
F32 = jnp.float32


def _params(*semantics):
    return pltpu.CompilerParams(dimension_semantics=semantics,
                                vmem_limit_bytes=V7X_VMEM_LIMIT_BYTES)


def _resident(shape):
    return pl.BlockSpec(shape, lambda *_: (0,) * len(shape), pipeline_mode=pl.Buffered(1))


def _rms(x, g):
    return x * lax.rsqrt(jnp.mean(x * x, axis=-1, keepdims=True) + EPS) * g


def _dot(a, b):
    return jnp.dot(a, b, preferred_element_type=F32)


def _dot_nt(a, b):
    return lax.dot_general(a, b, (((1,), (1,)), ((), ())), preferred_element_type=F32)


def _dot_tn(a, b):
    return lax.dot_general(a, b, (((0,), (0,)), ((), ())), preferred_element_type=F32)


def _ffn_body(h_ref, g_ref, wg_ref, wu_ref, wd_ref, gf_ref, o_ref, *, final_norm):
    x = h_ref[...]
    xn = _rms(x, g_ref[...]).astype(BF16)
    gate = _dot(xn, wg_ref[...])
    up = _dot(xn, wu_ref[...])
    a = (gate * jax.nn.sigmoid(gate) * up).astype(BF16)
    y = x + 0.5 * _dot(a, wd_ref[...])
    if final_norm:
        y = _rms(y, gf_ref[...])
    o_ref[...] = y


def _ffn(h, g, wg, wu, wd, g_final, *, layer, final_norm, tm=512):
    s, d = h.shape
    f = wg.shape[2]
    row = pl.BlockSpec((tm, d), lambda i: (i, 0))

    def one_layer(rows, cols):
        return pl.BlockSpec((None, rows, cols), lambda i: (layer, 0, 0), pipeline_mode=pl.Buffered(1))

    return pl.pallas_call(
        functools.partial(_ffn_body, final_norm=final_norm),
        out_shape=jax.ShapeDtypeStruct((s, d), F32),
        grid=(s // tm,),
        in_specs=[row, _resident((1, d)), one_layer(d, f), one_layer(d, f),
                  one_layer(f, d), _resident((1, d))],
        out_specs=row,
        compiler_params=_params("arbitrary"),
        name="ffn",
    )(h, g, wg, wu, wd, g_final)


CAST_BLOCK_BYTES = 6 * 1024 * 1024


def _cast_body(x_ref, o_ref):
    o_ref[...] = x_ref[...].astype(BF16)


def _to_bf16(w):
    layers, rows, cols = w.shape
    fits = [t for t in range(16, rows + 1, 16) if rows % t == 0 and t * cols * 4 <= CAST_BLOCK_BYTES]
    tr = max(fits) if fits else rows
    blk = pl.BlockSpec((1, tr, cols), lambda i, j: (i, j, 0))
    return pl.pallas_call(
        _cast_body,
        out_shape=jax.ShapeDtypeStruct(w.shape, BF16),
        grid=(layers, rows // tr),
        in_specs=[blk],
        out_specs=blk,
        compiler_params=_params("arbitrary", "arbitrary"),
        name="to_bf16",
    )(w)


def _conv_body(h_ref, g_ref, w1_ref, b1_ref, wdw_ref, bdw_ref, lng_ref, lnb_ref,
               w2_ref, b2_ref, o_ref, ext_ref, cv_ref, wb_ref, *, tm, rb, lc):
    d = h_ref.shape[1]

    @pl.when(pl.program_id(0) == 0)
    def _():
        ext_ref[0:CONV_HALO, :] = jnp.zeros((CONV_HALO, d), F32)
        for j in range(CONV_WIDTH):
            wb_ref[j] = jnp.broadcast_to(wdw_ref[j:j + 1, :], (SUBLANES, d))

    x = h_ref[...]
    u = _rms(x, g_ref[...]).astype(BF16)
    y = _dot(u, w1_ref[...]) + b1_ref[...]
    ext_ref[CONV_HALO:CONV_HALO + tm, :] = y[:, :d] * jax.nn.sigmoid(y[:, d:])

    shift = CONV_HALO - (CONV_WIDTH - 1)

    def rows(r, carry):
        base = pl.multiple_of(r * rb, rb)
        for c0 in range(0, d, lc):
            lanes = slice(c0, c0 + lc)
            acc = jnp.broadcast_to(bdw_ref[:, lanes], (rb, lc))
            for res in range(SUBLANES):
                taps = [j for j in range(CONV_WIDTH) if (shift + j) % SUBLANES == res]
                nrows = rb + (SUBLANES if res else 0)
                part = None
                for j in taps:
                    off = pl.multiple_of(base + (shift + j - res), SUBLANES)
                    slabs = ext_ref[pl.ds(off, nrows), lanes].reshape(nrows // SUBLANES, SUBLANES, lc)
                    term = wb_ref[j, :, lanes] * slabs
                    part = term if part is None else part + term
                acc = acc + part.reshape(nrows, lc)[res:res + rb]
            cv_ref[pl.ds(base, rb), lanes] = acc
        return carry

    lax.fori_loop(0, tm // rb, rows, 0)
    ext_ref[0:CONV_HALO, :] = ext_ref[tm:tm + CONV_HALO, :]

    c = cv_ref[...]
    mu = jnp.mean(c, axis=-1, keepdims=True)
    cc = c - mu
    var = jnp.mean(cc * cc, axis=-1, keepdims=True)
    z = cc * lax.rsqrt(var + EPS) * lng_ref[...] + lnb_ref[...]
    z = (z * jax.nn.sigmoid(z)).astype(BF16)
    o_ref[...] = x + _dot(z, w2_ref[...]) + b2_ref[...]


def _conv_mixer(h, g, w1, b1, wdw, bdw, lng, lnb, w2, b2, *, tm=512, rb=64, lc=128):
    s, d = h.shape
    row = pl.BlockSpec((tm, d), lambda i: (i, 0))
    return pl.pallas_call(
        functools.partial(_conv_body, tm=tm, rb=rb, lc=lc),
        out_shape=jax.ShapeDtypeStruct((s, d), F32),
        grid=(s // tm,),
        in_specs=[row, _resident((1, d)), _resident((d, 2 * d)), _resident((1, 2 * d)),
                  _resident((CONV_WIDTH, d)), _resident((1, d)), _resident((1, d)),
                  _resident((1, d)), _resident((d, d)), _resident((1, d))],
        out_specs=row,
        scratch_shapes=[pltpu.VMEM((tm + CONV_HALO, d), F32), pltpu.VMEM((tm, d), F32),
                        pltpu.VMEM((CONV_WIDTH, SUBLANES, d), F32)],
        compiler_params=_params("arbitrary"),
        name="conv_mixer",
    )(h, g, w1, b1, wdw, bdw, lng, lnb, w2, b2)


def _gla_tables():
    c = GLA_CHUNK
    r = np.arange(c)[:, None]
    k = np.arange(c)[None, :]
    mats = [(k <= r), (k > r)]
    masks = [(k == r)]
    for lvl in range(1, GLA_LEVELS + 1):
        blk, half = 1 << lvl, 1 << (lvl - 1)
        mid = (r // blk) * blk + half - 1
        second = (r % blk) >= half
        mats.append(np.where(second, (k > mid) & (k <= r), (k > r) & (k <= mid)))
        masks.append(((r // blk) == (k // blk)) & second & ((k % blk) < half))
    return (np.concatenate(mats, 0).astype(np.float32),
            np.concatenate(masks, 0).astype(np.float32))


def _gla_body(h_ref, g_ref, wq_ref, wk_ref, wv_ref, wr_ref, wa1_ref, wa2_ref, ba_ref, br_ref,
              ng_ref, wo_ref, tab_ref, msk_ref, o_ref,
              q_s, k_s, v_s, la_s, r_s, og_s, st_s, *, tm):
    c = GLA_CHUNK
    dk = wq_ref.shape[1]
    dv = wv_ref.shape[1]
    hk = dk // GLA_HEADS
    hv = dv // GLA_HEADS

    @pl.when(pl.program_id(0) == 0)
    def _():
        st_s[...] = jnp.zeros(st_s.shape, F32)

    x = h_ref[...]
    u = _rms(x, g_ref[...]).astype(BF16)
    q_s[...] = _dot(u, wq_ref[...]) * hk ** -0.5
    k_s[...] = _dot(u, wk_ref[...])
    v_s[...] = _dot(u, wv_ref[...]).astype(BF16)
    lo = _dot(u, wa1_ref[...]).astype(BF16)
    la_s[...] = jax.nn.log_sigmoid(_dot(lo, wa2_ref[...]) + ba_ref[...]) / GLA_TAU
    rr = _dot(u, wr_ref[...]) + br_ref[...]
    r_s[...] = rr * jax.nn.sigmoid(rr)

    tab = tab_ref[...]

    def chunk(ci, carry):
        rows = pl.ds(pl.multiple_of(ci * c, c), c)
        la = la_s[rows, :]
        la1 = la.astype(BF16)
        rem = la - la1.astype(F32)
        la2 = rem.astype(BF16)
        la3 = (rem - la2.astype(F32)).astype(BF16)
        e = jnp.exp(_dot(tab, jnp.concatenate([la1, la2, la3], axis=0)))
        eb = e[0:c]
        ekl = e[c:2 * c]
        elast = e[c - 1:c]
        q = q_s[rows, :]
        k = k_s[rows, :]
        v = v_s[rows, :]
        for hd in range(GLA_HEADS):
            ks = slice(hd * hk, (hd + 1) * hk)
            vs = slice(hd * hv, (hd + 1) * hv)
            qh, kh, vh = q[:, ks], k[:, ks], v[:, vs]
            att = msk_ref[0:c, :] * _dot_nt(qh.astype(BF16), kh.astype(BF16))
            for lvl in range(1, GLA_LEVELS + 1):
                fl = e[(1 + lvl) * c:(2 + lvl) * c, ks]
                att = att + msk_ref[lvl * c:(lvl + 1) * c, :] * _dot_nt(
                    (qh * fl).astype(BF16), (kh * fl).astype(BF16))
            st = st_s[hd]
            o = _dot(att.astype(BF16), vh) + _dot_nt((qh * eb[:, ks]).astype(BF16), st.astype(BF16))
            st_s[hd] = elast[:, ks] * st + _dot_tn(vh, (kh * ekl[:, ks]).astype(BF16))
            on = _rms(o, ng_ref[...])
            og_s[rows, vs] = (on * r_s[rows, vs]).astype(BF16)
        return carry

    lax.fori_loop(0, tm // c, chunk, 0, unroll=True)
    o_ref[...] = x + _dot(og_s[...], wo_ref[...])


def _gla_mixer(h, g, wq, wk, wv, wr, wa1, wa2, ba, br, ng, wo, *, tm=256):
    s, d = h.shape
    dk, dv = wq.shape[1], wv.shape[1]
    tab, msk = _gla_tables()
    tab = jnp.asarray(np.concatenate([tab, tab, tab], axis=1), BF16)
    msk = jnp.asarray(msk, F32)
    row = pl.BlockSpec((tm, d), lambda i: (i, 0))
    return pl.pallas_call(
        functools.partial(_gla_body, tm=tm),
        out_shape=jax.ShapeDtypeStruct((s, d), F32),
        grid=(s // tm,),
        in_specs=[row, _resident((1, d)), _resident(wq.shape), _resident(wk.shape),
                  _resident(wv.shape), _resident(wr.shape), _resident(wa1.shape),
                  _resident(wa2.shape), _resident((1, dk)), _resident((1, dv)),
                  _resident((1, dv // GLA_HEADS)), _resident(wo.shape),
                  _resident(tab.shape), _resident(msk.shape)],
        out_specs=row,
        scratch_shapes=[pltpu.VMEM((tm, dk), F32), pltpu.VMEM((tm, dk), F32),
                        pltpu.VMEM((tm, dv), BF16), pltpu.VMEM((tm, dk), F32),
                        pltpu.VMEM((tm, dv), F32), pltpu.VMEM((tm, dv), BF16),
                        pltpu.VMEM((GLA_HEADS, dv // GLA_HEADS, dk // GLA_HEADS), F32)],
        compiler_params=_params("arbitrary"),
        name="gla_mixer",
    )(h, g, wq, wk, wv, wr, wa1, wa2, ba, br, ng, wo, tab, msk)


def _moba_proj_body(h_ref, g_ref, wq_ref, wk_ref, wvt_ref, q_ref, k_ref, vt_ref, sel_ref, km_s,
                    *, scale, nb):
    i = pl.program_id(0)
    blk, d = h_ref.shape
    dh = d // MOBA_HEADS

    @pl.when(i == 0)
    def _():
        km_s[...] = jnp.zeros(km_s.shape, F32)

    u = _rms(h_ref[...], g_ref[...]).astype(BF16)
    q = (_dot(u, wq_ref[...]) * scale).astype(BF16)
    q_ref[...] = q
    k = _dot(u, wk_ref[...])
    k_ref[...] = k.astype(BF16)
    vt = _dot_nt(wvt_ref[...], u).astype(BF16)
    for hd in range(MOBA_HEADS):
        base = hd * (dh + MOBA_ONES_ROWS)
        vt_ref[base:base + dh, :] = vt[hd * dh:(hd + 1) * dh]
        vt_ref[base + dh:base + dh + MOBA_ONES_ROWS, :] = jnp.ones((MOBA_ONES_ROWS, blk), BF16)

    blk_id = lax.broadcasted_iota(jnp.int32, (nb, blk), 0)
    kmb = km_s[...].astype(BF16)
    for hd in range(MOBA_HEADS):
        cols = slice(hd * dh, (hd + 1) * dh)
        gate = jnp.where(blk_id < i, _dot_nt(kmb[:, cols], q[:, cols]), -jnp.inf)
        sel = jnp.where(blk_id == i, 1.0, 0.0)
        for _ in range(MOBA_TOPK):
            best = jnp.max(gate, axis=0, keepdims=True)
            idx = jnp.min(jnp.where(gate == best, blk_id, nb), axis=0, keepdims=True)
            hit = blk_id == idx
            sel = jnp.where(hit & (best > -jnp.inf), 1.0, sel)
            gate = jnp.where(hit, -jnp.inf, gate)
        sel_ref[0, hd * nb:(hd + 1) * nb, :] = sel
    km_s[pl.ds(i, 1), :] = jnp.mean(k, axis=0, keepdims=True)


def _moba_proj(h, g, wq, wk, wvt):
    s, d = h.shape
    tm = MOBA_BLOCK
    nb = s // tm
    dv = d + MOBA_HEADS * MOBA_ONES_ROWS
    row = pl.BlockSpec((tm, d), lambda i: (i, 0))
    return pl.pallas_call(
        functools.partial(_moba_proj_body, scale=(d // MOBA_HEADS) ** -0.5, nb=nb),
        out_shape=[jax.ShapeDtypeStruct((s, d), BF16), jax.ShapeDtypeStruct((s, d), BF16),
                   jax.ShapeDtypeStruct((dv, s), BF16),
                   jax.ShapeDtypeStruct((nb, MOBA_HEADS * nb, tm), F32)],
        grid=(nb,),
        in_specs=[row, _resident((1, d)), _resident((d, d)), _resident((d, d)), _resident((d, d))],
        out_specs=[row, row, pl.BlockSpec((dv, tm), lambda i: (0, i)),
                   pl.BlockSpec((1, MOBA_HEADS * nb, tm), lambda i: (i, 0, 0))],
        scratch_shapes=[pltpu.VMEM((nb, d), F32)],
        compiler_params=_params("arbitrary"),
        name="moba_proj",
    )(h, g, wq, wk, wvt)


def _moba_attn_body(slope_ref, qa_ref, qb_ref, k_ref, vt_ref, sela_ref, selb_ref, oa_ref, ob_ref,
                    bias_ref, q_s, sel_s, m_s, acc_s, sa_ref, sb_ref, sc_ref, pa_ref, pb_ref,
                    *, nb, sub):
    blk = MOBA_BLOCK
    span = sub * blk
    hd = pl.program_id(0)
    p = pl.program_id(1)
    slope = slope_ref[hd]
    dh = qa_ref.shape[1]
    updates = nb // sub + 1
    first_b = (p + sub) // sub

    @pl.when(p == 0)
    def _():
        r = lax.broadcasted_iota(jnp.int32, (blk, blk), 0)
        c = lax.broadcasted_iota(jnp.int32, (blk, blk), 1)
        rel = (c - r).astype(F32)
        for g in range(sub):
            plain = slope * (rel - float(g * blk))
            bias_ref[g] = plain
            bias_ref[sub + g] = jnp.where(c >= r, plain, jnp.inf)

    q_s[0] = qa_ref[...]
    q_s[1] = qb_ref[...]
    sel_s[0] = sela_ref[0]
    sel_s[1] = selb_ref[0]
    m_s[...] = jnp.full(m_s.shape, MOBA_MIN_SCORE, F32)
    acc_s[...] = jnp.zeros(acc_s.shape, F32)

    def plan(u):
        is_b = u >= first_b
        slot = is_b.astype(jnp.int32)
        tile = jnp.where(is_b, nb - 1 - p, p)
        kb = jnp.where(is_b, u - first_b, u) * sub
        return slot, tile, kb

    def keys_of(kb):
        return pl.ds(pl.multiple_of(kb * blk, span), span)

    def scores(u, dst_ref):
        slot, _, kb = plan(u)
        dst_ref[...] = _dot_nt(k_ref[keys_of(kb), :], q_s[slot])

    def softmax(u, src_ref, dst_ref):
        slot, tile, kb = plan(u)
        m = m_s[slot]
        off = slope * ((tile - kb) * blk).astype(F32)
        chosen = [sel_s[slot, pl.ds(kb + g, 1), :] > 0.0 for g in range(sub)]
        top = None
        for g in range(sub):
            rows = slice(g * blk, (g + 1) * blk)
            biased = src_ref[rows, :] - bias_ref[jnp.where(kb + g == tile, sub + g, g)]
            mg = jnp.where(chosen[g], jnp.max(biased, axis=0, keepdims=True), -jnp.inf)
            top = mg if top is None else jnp.maximum(top, mg)
        m_new = jnp.maximum(m, top - off)
        m_s[slot] = m_new
        zero = jnp.minimum(p, 0) * blk
        for g in range(sub):
            shift = jnp.where(chosen[g], m_new + off, jnp.inf)
            rows = pl.ds(pl.multiple_of(zero + g * blk, blk), blk)
            table = bias_ref[jnp.where(kb + g == tile, sub + g, g)]
            dst_ref[g * blk:(g + 1) * blk, :] = jnp.exp(src_ref[rows, :] - table - shift).astype(BF16)
        return jnp.exp(m - m_new)

    def add_values(u, p_ref):
        slot, _, kb = plan(u)
        acc_s[slot] = acc_s[slot] + _dot(vt_ref[:, keys_of(kb)], p_ref[...])

    stage_s = (sa_ref, sb_ref, sc_ref)
    stage_p = (pa_ref, pb_ref)
    scores(0, sa_ref)
    for u in range(updates):
        if u + 1 < updates:
            scores(u + 1, stage_s[(u + 1) % 3])
        alpha = softmax(u, stage_s[u % 3], stage_p[u % 2])
        if u > 0:
            add_values(u - 1, stage_p[(u - 1) % 2])
        slot = plan(u)[0]
        acc_s[slot] = alpha * acc_s[slot]
    add_values(updates - 1, stage_p[(updates - 1) % 2])

    for slot, o_ref in ((0, oa_ref), (1, ob_ref)):
        acc = acc_s[slot]
        o_ref[...] = (acc[0:dh] / acc[dh:dh + 1]).T.astype(BF16)


def _moba_attn(q, k, vt, sel, slopes, *, sub=4):
    s, d = q.shape
    dh = d // MOBA_HEADS
    blk = MOBA_BLOCK
    nb = s // blk
    half = nb // 2
    sub = min(sub, half)
    assert nb % (2 * sub) == 0
    scores = pltpu.VMEM((sub * blk, blk), F32)
    probs = pltpu.VMEM((sub * blk, blk), BF16)
    tile_a = lambda h, p, _: (p, h)
    tile_b = lambda h, p, _: (nb - 1 - p, h)
    return pl.pallas_call(
        functools.partial(_moba_attn_body, nb=nb, sub=sub),
        out_shape=[jax.ShapeDtypeStruct((s // 2, d), BF16)] * 2,
        grid_spec=pltpu.PrefetchScalarGridSpec(
            num_scalar_prefetch=1,
            grid=(MOBA_HEADS, half),
            in_specs=[pl.BlockSpec((blk, dh), tile_a),
                      pl.BlockSpec((blk, dh), tile_b),
                      pl.BlockSpec((s, dh), lambda h, p, _: (0, h)),
                      pl.BlockSpec((dh + MOBA_ONES_ROWS, s), lambda h, p, _: (h, 0)),
                      pl.BlockSpec((1, nb, blk), lambda h, p, _: (p, h, 0)),
                      pl.BlockSpec((1, nb, blk), lambda h, p, _: (nb - 1 - p, h, 0))],
            out_specs=[pl.BlockSpec((blk, dh), tile_a),
                       pl.BlockSpec((blk, dh), lambda h, p, _: (half - 1 - p, h))],
            scratch_shapes=[pltpu.VMEM((2 * sub, blk, blk), F32),
                            pltpu.VMEM((2, blk, dh), BF16),
                            pltpu.VMEM((2, nb, blk), F32),
                            pltpu.VMEM((2, 1, blk), F32),
                            pltpu.VMEM((2, dh + MOBA_ONES_ROWS, blk), F32),
                            scores, scores, scores, probs, probs]),
        compiler_params=_params("arbitrary", "arbitrary"),
        name="moba_attn",
    )(slopes, q, q, k, vt, sel, sel)


def _proj_residual_body(h_ref, lo_ref, hi_ref, w_ref, o_ref, *, half_steps):
    a = jnp.where(pl.program_id(0) < half_steps, lo_ref[...], hi_ref[...])
    o_ref[...] = h_ref[...] + _dot(a, w_ref[...])


def _proj_residual(h, a_lo, a_hi, w, *, tm=512):
    s, d = h.shape
    half_steps = a_lo.shape[0] // tm
    row = pl.BlockSpec((tm, d), lambda i: (i, 0))
    return pl.pallas_call(
        functools.partial(_proj_residual_body, half_steps=half_steps),
        out_shape=jax.ShapeDtypeStruct((s, d), F32),
        grid=(s // tm,),
        in_specs=[row,
                  pl.BlockSpec((tm, a_lo.shape[1]), lambda i: (jnp.minimum(i, half_steps - 1), 0)),
                  pl.BlockSpec((tm, a_hi.shape[1]), lambda i: (jnp.maximum(i - half_steps, 0), 0)),
                  _resident(w.shape)],
        out_specs=row,
        compiler_params=_params("arbitrary"),
        name="proj_residual",
    )(h, a_lo, a_hi, w)


def _moba_mixer(h, g, wq, wk, wvt, wo):
    s, d = h.shape
    q, k, vt, sel = _moba_proj(h, g, wq, wk, wvt)
    heads = jnp.arange(1, MOBA_HEADS + 1, dtype=F32)
    slopes = 2.0 ** (-8.0 * heads / MOBA_HEADS)
    o_lo, o_hi = _moba_attn(q, k, vt, sel, slopes)
    return _proj_residual(h, o_lo, o_hi, wo)


def kernel(x, g_ffn1, g_mix, g_ffn2, g_final, w1_gate, w1_up, w1_down, w2_gate, w2_up, w2_down, cv_w_pw1, cv_b_pw1, cv_w_dw, cv_b_dw, cv_ln_g, cv_ln_b, cv_w_pw2, cv_b_pw2, gla_w_q, gla_w_k, gla_w_v, gla_w_a1, gla_w_a2, gla_b_a, gla_w_r, gla_b_r, gla_norm_g, gla_w_o, mb_w_q, mb_w_k, mb_w_v, mb_w_o):
    batch, seq, d = x.shape
    depth = g_ffn1.shape[0]
    assert batch == 1
    bf = lambda w: w.astype(BF16)
    vec = lambda b: b.reshape(1, -1)
    h = x.reshape(seq, d)
    gf = vec(g_final)
    ffn1 = [_to_bf16(w) for w in (w1_gate, w1_up, w1_down)]
    ffn2 = [_to_bf16(w) for w in (w2_gate, w2_up, w2_down)]
    cv_pw1, cv_pw2 = _to_bf16(cv_w_pw1), _to_bf16(cv_w_pw2)
    gla_q, gla_k, gla_v, gla_r, gla_o = [
        _to_bf16(w) for w in (gla_w_q, gla_w_k, gla_w_v, gla_w_r, gla_w_o)]
    mb_q, mb_k, mb_o = [_to_bf16(w) for w in (mb_w_q, mb_w_k, mb_w_o)]
    for i in range(depth):
        h = _ffn(h, vec(g_ffn1[i]), *ffn1, gf, layer=i, final_norm=False)
        kind, j = i % 3, i // 3
        gm = vec(g_mix[i])
        if kind == 0:
            h = _conv_mixer(h, gm, cv_pw1[j], vec(cv_b_pw1[j]), cv_w_dw[j], vec(cv_b_dw[j]),
                            vec(cv_ln_g[j]), vec(cv_ln_b[j]), cv_pw2[j], vec(cv_b_pw2[j]))
        elif kind == 1:
            rank = gla_w_a1.shape[2]
            wa1 = jnp.pad(gla_w_a1[j], ((0, 0), (0, GLA_RANK_PAD - rank)))
            wa2 = jnp.pad(gla_w_a2[j], ((0, GLA_RANK_PAD - rank), (0, 0)))
            h = _gla_mixer(h, gm, gla_q[j], gla_k[j], gla_v[j], gla_r[j],
                           bf(wa1), bf(wa2), vec(gla_b_a[j]), vec(gla_b_r[j]),
                           vec(gla_norm_g[j]), gla_o[j])
        else:
            h = _moba_mixer(h, gm, mb_q[j], mb_k[j], bf(mb_w_v[j].T), mb_o[j])
        h = _ffn(h, vec(g_ffn2[i]), *ffn2, gf, layer=i, final_norm=(i == depth - 1))
    return h.reshape(batch, seq, d)
```

```python
import functools

import jax
import jax.numpy as jnp
import numpy as np
from jax import lax
from jax.experimental import pallas as pl
from jax.experimental.pallas import tpu as pltpu

EPS = 1e-6
CONV_WIDTH = 31
GLA_HEADS = 4
GLA_TAU = 16.0
GLA_CHUNK = 64
GLA_LEVELS = 6
GLA_RANK_PAD = 128
MOBA_HEADS = 8
MOBA_BLOCK = 256
MOBA_TOPK = 3
MOBA_ONES_ROWS = 16
MOBA_MIN_SCORE = float(np.finfo(np.float32).min) / 2

V7X_VMEM_LIMIT_BYTES = 56 * 1024 * 1024
SUBLANES = 8
CONV_HALO = 32

BF16 = jnp.bfloat16
F32 = jnp.float32


def _params(*semantics):
    return pltpu.CompilerParams(dimension_semantics=semantics,
                                vmem_limit_bytes=V7X_VMEM_LIMIT_BYTES)


def _resident(shape):
    return pl.BlockSpec(shape, lambda *_: (0,) * len(shape), pipeline_mode=pl.Buffered(1))


def _rms(x, g):
    return x * lax.rsqrt(jnp.mean(x * x, axis=-1, keepdims=True) + EPS) * g


def _dot(a, b):
    return jnp.dot(a, b, preferred_element_type=F32)


def _dot_nt(a, b):
    return lax.dot_general(a, b, (((1,), (1,)), ((), ())), preferred_element_type=F32)


def _dot_tn(a, b):
    return lax.dot_general(a, b, (((0,), (0,)), ((), ())), preferred_element_type=F32)


def _ffn_body(h_ref, g_ref, wg_ref, wu_ref, wd_ref, gf_ref, o_ref, *, final_norm):
    x = h_ref[...]
    xn = _rms(x, g_ref[...]).astype(BF16)
    gate = _dot(xn, wg_ref[...])
    up = _dot(xn, wu_ref[...])
    a = (gate * jax.nn.sigmoid(gate) * up).astype(BF16)
    y = x + 0.5 * _dot(a, wd_ref[...])
    if final_norm:
        y = _rms(y, gf_ref[...])
    o_ref[...] = y


def _ffn(h, g, wg, wu, wd, g_final, *, layer, final_norm, tm=512):
    s, d = h.shape
    f = wg.shape[2]
    row = pl.BlockSpec((tm, d), lambda i: (i, 0))

    def one_layer(rows, cols):
        return pl.BlockSpec((None, rows, cols), lambda i: (layer, 0, 0), pipeline_mode=pl.Buffered(1))

    return pl.pallas_call(
        functools.partial(_ffn_body, final_norm=final_norm),
        out_shape=jax.ShapeDtypeStruct((s, d), F32),
        grid=(s // tm,),
        in_specs=[row, _resident((1, d)), one_layer(d, f), one_layer(d, f),
                  one_layer(f, d), _resident((1, d))],
        out_specs=row,
        compiler_params=_params("arbitrary"),
        name="ffn",
    )(h, g, wg, wu, wd, g_final)


CAST_BLOCK_BYTES = 6 * 1024 * 1024


def _cast_body(x_ref, o_ref):
    o_ref[...] = x_ref[...].astype(BF16)


def _to_bf16(w):
    layers, rows, cols = w.shape
    fits = [t for t in range(16, rows + 1, 16) if rows % t == 0 and t * cols * 4 <= CAST_BLOCK_BYTES]
    tr = max(fits) if fits else rows
    blk = pl.BlockSpec((1, tr, cols), lambda i, j: (i, j, 0))
    return pl.pallas_call(
        _cast_body,
        out_shape=jax.ShapeDtypeStruct(w.shape, BF16),
        grid=(layers, rows // tr),
        in_specs=[blk],
        out_specs=blk,
        compiler_params=_params("arbitrary", "arbitrary"),
        name="to_bf16",
    )(w)


def _conv_body(h_ref, g_ref, w1_ref, b1_ref, wdw_ref, bdw_ref, lng_ref, lnb_ref,
               w2_ref, b2_ref, o_ref, ext_ref, cv_ref, wb_ref, *, tm, rb, lc):
    d = h_ref.shape[1]

    @pl.when(pl.program_id(0) == 0)
    def _():
        ext_ref[0:CONV_HALO, :] = jnp.zeros((CONV_HALO, d), F32)
        for j in range(CONV_WIDTH):
            wb_ref[j] = jnp.broadcast_to(wdw_ref[j:j + 1, :], (SUBLANES, d))

    x = h_ref[...]
    u = _rms(x, g_ref[...]).astype(BF16)
    y = _dot(u, w1_ref[...]) + b1_ref[...]
    ext_ref[CONV_HALO:CONV_HALO + tm, :] = y[:, :d] * jax.nn.sigmoid(y[:, d:])

    shift = CONV_HALO - (CONV_WIDTH - 1)

    def rows(r, carry):
        base = pl.multiple_of(r * rb, rb)
        for c0 in range(0, d, lc):
            lanes = slice(c0, c0 + lc)
            acc = jnp.broadcast_to(bdw_ref[:, lanes], (rb, lc))
            for res in range(SUBLANES):
                taps = [j for j in range(CONV_WIDTH) if (shift + j) % SUBLANES == res]
                nrows = rb + (SUBLANES if res else 0)
                part = None
                for j in taps:
                    off = pl.multiple_of(base + (shift + j - res), SUBLANES)
                    slabs = ext_ref[pl.ds(off, nrows), lanes].reshape(nrows // SUBLANES, SUBLANES, lc)
                    term = wb_ref[j, :, lanes] * slabs
                    part = term if part is None else part + term
                acc = acc + part.reshape(nrows, lc)[res:res + rb]
            cv_ref[pl.ds(base, rb), lanes] = acc
        return carry

    lax.fori_loop(0, tm // rb, rows, 0)
    ext_ref[0:CONV_HALO, :] = ext_ref[tm:tm + CONV_HALO, :]

    c = cv_ref[...]
    mu = jnp.mean(c, axis=-1, keepdims=True)
    cc = c - mu
    var = jnp.mean(cc * cc, axis=-1, keepdims=True)
    z = cc * lax.rsqrt(var + EPS) * lng_ref[...] + lnb_ref[...]
    z = (z * jax.nn.sigmoid(z)).astype(BF16)
    o_ref[...] = x + _dot(z, w2_ref[...]) + b2_ref[...]


def _conv_mixer(h, g, w1, b1, wdw, bdw, lng, lnb, w2, b2, *, tm=512, rb=64, lc=128):
    s, d = h.shape
    row = pl.BlockSpec((tm, d), lambda i: (i, 0))
    return pl.pallas_call(
        functools.partial(_conv_body, tm=tm, rb=rb, lc=lc),
        out_shape=jax.ShapeDtypeStruct((s, d), F32),
        grid=(s // tm,),
        in_specs=[row, _resident((1, d)), _resident((d, 2 * d)), _resident((1, 2 * d)),
                  _resident((CONV_WIDTH, d)), _resident((1, d)), _resident((1, d)),
                  _resident((1, d)), _resident((d, d)), _resident((1, d))],
        out_specs=row,
        scratch_shapes=[pltpu.VMEM((tm + CONV_HALO, d), F32), pltpu.VMEM((tm, d), F32),
                        pltpu.VMEM((CONV_WIDTH, SUBLANES, d), F32)],
        compiler_params=_params("arbitrary"),
        name="conv_mixer",
    )(h, g, w1, b1, wdw, bdw, lng, lnb, w2, b2)


def _gla_tables():
    c = GLA_CHUNK
    r = np.arange(c)[:, None]
    k = np.arange(c)[None, :]
    mats = [(k <= r), (k > r)]
    masks = [(k == r)]
    for lvl in range(1, GLA_LEVELS + 1):
        blk, half = 1 << lvl, 1 << (lvl - 1)
        mid = (r // blk) * blk + half - 1
        second = (r % blk) >= half
        mats.append(np.where(second, (k > mid) & (k <= r), (k > r) & (k <= mid)))
        masks.append(((r // blk) == (k // blk)) & second & ((k % blk) < half))
    return (np.concatenate(mats, 0).astype(np.float32),
            np.concatenate(masks, 0).astype(np.float32))


def _gla_body(h_ref, g_ref, wq_ref, wk_ref, wv_ref, wr_ref, wa1_ref, wa2_ref, ba_ref, br_ref,
              ng_ref, wo_ref, tab_ref, msk_ref, o_ref,
              q_s, k_s, v_s, la_s, r_s, og_s, st_s, *, tm):
    c = GLA_CHUNK
    dk = wq_ref.shape[1]
    dv = wv_ref.shape[1]
    hk = dk // GLA_HEADS
    hv = dv // GLA_HEADS

    @pl.when(pl.program_id(0) == 0)
    def _():
        st_s[...] = jnp.zeros(st_s.shape, F32)

    x = h_ref[...]
    u = _rms(x, g_ref[...]).astype(BF16)
    q_s[...] = _dot(u, wq_ref[...]) * hk ** -0.5
    k_s[...] = _dot(u, wk_ref[...])
    v_s[...] = _dot(u, wv_ref[...]).astype(BF16)
    lo = _dot(u, wa1_ref[...]).astype(BF16)
    la_s[...] = jax.nn.log_sigmoid(_dot(lo, wa2_ref[...]) + ba_ref[...]) / GLA_TAU
    rr = _dot(u, wr_ref[...]) + br_ref[...]
    r_s[...] = rr * jax.nn.sigmoid(rr)

    tab = tab_ref[...]

    def chunk(ci, carry):
        rows = pl.ds(pl.multiple_of(ci * c, c), c)
        la = la_s[rows, :]
        la1 = la.astype(BF16)
        rem = la - la1.astype(F32)
        la2 = rem.astype(BF16)
        la3 = (rem - la2.astype(F32)).astype(BF16)
        e = jnp.exp(_dot(tab, jnp.concatenate([la1, la2, la3], axis=0)))
        eb = e[0:c]
        ekl = e[c:2 * c]
        elast = e[c - 1:c]
        q = q_s[rows, :]
        k = k_s[rows, :]
        v = v_s[rows, :]
        for hd in range(GLA_HEADS):
            ks = slice(hd * hk, (hd + 1) * hk)
            vs = slice(hd * hv, (hd + 1) * hv)
            qh, kh, vh = q[:, ks], k[:, ks], v[:, vs]
            att = msk_ref[0:c, :] * _dot_nt(qh.astype(BF16), kh.astype(BF16))
            for lvl in range(1, GLA_LEVELS + 1):
                fl = e[(1 + lvl) * c:(2 + lvl) * c, ks]
                att = att + msk_ref[lvl * c:(lvl + 1) * c, :] * _dot_nt(
                    (qh * fl).astype(BF16), (kh * fl).astype(BF16))
            st = st_s[hd]
            o = _dot(att.astype(BF16), vh) + _dot_nt((qh * eb[:, ks]).astype(BF16), st.astype(BF16))
            st_s[hd] = elast[:, ks] * st + _dot_tn(vh, (kh * ekl[:, ks]).astype(BF16))
            on = _rms(o, ng_ref[...])
            og_s[rows, vs] = (on * r_s[rows, vs]).astype(BF16)
        return carry

    lax.fori_loop(0, tm // c, chunk, 0, unroll=True)
    o_ref[...] = x + _dot(og_s[...], wo_ref[...])


def _gla_mixer(h, g, wq, wk, wv, wr, wa1, wa2, ba, br, ng, wo, *, tm=256):
    s, d = h.shape
    dk, dv = wq.shape[1], wv.shape[1]
    tab, msk = _gla_tables()
    tab = jnp.asarray(np.concatenate([tab, tab, tab], axis=1), BF16)
    msk = jnp.asarray(msk, F32)
    row = pl.BlockSpec((tm, d), lambda i: (i, 0))
    return pl.pallas_call(
        functools.partial(_gla_body, tm=tm),
        out_shape=jax.ShapeDtypeStruct((s, d), F32),
        grid=(s // tm,),
        in_specs=[row, _resident((1, d)), _resident(wq.shape), _resident(wk.shape),
                  _resident(wv.shape), _resident(wr.shape), _resident(wa1.shape),
                  _resident(wa2.shape), _resident((1, dk)), _resident((1, dv)),
                  _resident((1, dv // GLA_HEADS)), _resident(wo.shape),
                  _resident(tab.shape), _resident(msk.shape)],
        out_specs=row,
        scratch_shapes=[pltpu.VMEM((tm, dk), F32), pltpu.VMEM((tm, dk), F32),
                        pltpu.VMEM((tm, dv), BF16), pltpu.VMEM((tm, dk), F32),
                        pltpu.VMEM((tm, dv), F32), pltpu.VMEM((tm, dv), BF16),
                        pltpu.VMEM((GLA_HEADS, dv // GLA_HEADS, dk // GLA_HEADS), F32)],
        compiler_params=_params("arbitrary"),
        name="gla_mixer",
    )(h, g, wq, wk, wv, wr, wa1, wa2, ba, br, ng, wo, tab, msk)


def _moba_proj_body(h_ref, g_ref, wqt_ref, wk_ref, wvt_ref, qt_ref, k_ref, vt_ref, sel_ref, km_s,
                    *, scale, nb):
    i = pl.program_id(0)
    blk, d = h_ref.shape
    dh = d // MOBA_HEADS

    @pl.when(i == 0)
    def _():
        km_s[...] = jnp.zeros(km_s.shape, F32)

    u = _rms(h_ref[...], g_ref[...]).astype(BF16)
    qt = (_dot_nt(wqt_ref[...], u) * scale).astype(BF16)
    qt_ref[...] = qt
    k = _dot(u, wk_ref[...])
    k_ref[...] = k.astype(BF16)
    vt = _dot_nt(wvt_ref[...], u).astype(BF16)
    for hd in range(MOBA_HEADS):
        base = hd * (dh + MOBA_ONES_ROWS)
        vt_ref[base:base + dh, :] = vt[hd * dh:(hd + 1) * dh]
        vt_ref[base + dh:base + dh + MOBA_ONES_ROWS, :] = jnp.ones((MOBA_ONES_ROWS, blk), BF16)

    blk_id = lax.broadcasted_iota(jnp.int32, (nb, blk), 0)
    kmb = km_s[...].astype(BF16)
    for hd in range(MOBA_HEADS):
        cols = slice(hd * dh, (hd + 1) * dh)
        gate = jnp.where(blk_id < i, _dot(kmb[:, cols], qt[cols, :]), -jnp.inf)
        sel = jnp.where(blk_id == i, 1.0, 0.0)
        for _ in range(MOBA_TOPK):
            best = jnp.max(gate, axis=0, keepdims=True)
            idx = jnp.min(jnp.where(gate == best, blk_id, nb), axis=0, keepdims=True)
            hit = blk_id == idx
            sel = jnp.where(hit & (best > -jnp.inf), 1.0, sel)
            gate = jnp.where(hit, -jnp.inf, gate)
        sel_ref[0, hd * nb:(hd + 1) * nb, :] = sel
    km_s[pl.ds(i, 1), :] = jnp.mean(k, axis=0, keepdims=True)


def _moba_proj(h, g, wqt, wk, wvt):
    s, d = h.shape
    tm = MOBA_BLOCK
    nb = s // tm
    dv = d + MOBA_HEADS * MOBA_ONES_ROWS
    row = pl.BlockSpec((tm, d), lambda i: (i, 0))
    return pl.pallas_call(
        functools.partial(_moba_proj_body, scale=(d // MOBA_HEADS) ** -0.5, nb=nb),
        out_shape=[jax.ShapeDtypeStruct((d, s), BF16), jax.ShapeDtypeStruct((s, d), BF16),
                   jax.ShapeDtypeStruct((dv, s), BF16),
                   jax.ShapeDtypeStruct((nb, MOBA_HEADS * nb, tm), F32)],
        grid=(nb,),
        in_specs=[row, _resident((1, d)), _resident((d, d)), _resident((d, d)), _resident((d, d))],
        out_specs=[pl.BlockSpec((d, tm), lambda i: (0, i)), row,
                   pl.BlockSpec((dv, tm), lambda i: (0, i)),
                   pl.BlockSpec((1, MOBA_HEADS * nb, tm), lambda i: (i, 0, 0))],
        scratch_shapes=[pltpu.VMEM((nb, d), F32)],
        compiler_params=_params("arbitrary"),
        name="moba_proj",
    )(h, g, wqt, wk, wvt)


def _moba_attn_body(slope_ref, qa_ref, qb_ref, k_ref, vt_ref, sela_ref, selb_ref, oa_ref, ob_ref,
                    bias_ref, q_s, sel_s, m_s, acc_s, sa_ref, sb_ref, sc_ref, pa_ref, pb_ref,
                    *, nb, sub):
    blk = MOBA_BLOCK
    span = sub * blk
    hd = pl.program_id(0)
    p = pl.program_id(1)
    slope = slope_ref[hd]
    dh = qa_ref.shape[0]
    updates = nb // sub + 1
    first_b = (p + sub) // sub

    @pl.when(p == 0)
    def _():
        r = lax.broadcasted_iota(jnp.int32, (blk, blk), 0)
        c = lax.broadcasted_iota(jnp.int32, (blk, blk), 1)
        rel = (c - r).astype(F32)
        for g in range(sub):
            plain = slope * (rel - float(g * blk))
            bias_ref[g] = plain
            bias_ref[sub + g] = jnp.where(c >= r, plain, jnp.inf)

    q_s[0] = qa_ref[...]
    q_s[1] = qb_ref[...]
    sel_s[0] = sela_ref[0]
    sel_s[1] = selb_ref[0]
    m_s[...] = jnp.full(m_s.shape, MOBA_MIN_SCORE, F32)
    acc_s[...] = jnp.zeros(acc_s.shape, F32)

    def plan(u):
        is_b = u >= first_b
        slot = is_b.astype(jnp.int32)
        tile = jnp.where(is_b, nb - 1 - p, p)
        kb = jnp.where(is_b, u - first_b, u) * sub
        return slot, tile, kb

    def keys_of(kb):
        return pl.ds(pl.multiple_of(kb * blk, span), span)

    def scores(u, dst_ref):
        slot, _, kb = plan(u)
        dst_ref[...] = _dot(k_ref[keys_of(kb), :], q_s[slot])

    def softmax(u, src_ref, dst_ref):
        slot, tile, kb = plan(u)
        m = m_s[slot]
        off = slope * ((tile - kb) * blk).astype(F32)
        chosen = [sel_s[slot, pl.ds(kb + g, 1), :] > 0.0 for g in range(sub)]
        top = None
        for g in range(sub):
            rows = slice(g * blk, (g + 1) * blk)
            biased = src_ref[rows, :] - bias_ref[jnp.where(kb + g == tile, sub + g, g)]
            mg = jnp.where(chosen[g], jnp.max(biased, axis=0, keepdims=True), -jnp.inf)
            top = mg if top is None else jnp.maximum(top, mg)
        m_new = jnp.maximum(m, top - off)
        m_s[slot] = m_new
        zero = jnp.minimum(p, 0) * blk
        for g in range(sub):
            shift = jnp.where(chosen[g], m_new + off, jnp.inf)
            rows = pl.ds(pl.multiple_of(zero + g * blk, blk), blk)
            table = bias_ref[jnp.where(kb + g == tile, sub + g, g)]
            dst_ref[g * blk:(g + 1) * blk, :] = jnp.exp(src_ref[rows, :] - table - shift).astype(BF16)
        return jnp.exp(m - m_new)

    def add_values(u, p_ref):
        slot, _, kb = plan(u)
        acc_s[slot] = acc_s[slot] + _dot(vt_ref[:, keys_of(kb)], p_ref[...])

    stage_s = (sa_ref, sb_ref, sc_ref)
    stage_p = (pa_ref, pb_ref)
    scores(0, sa_ref)
    for u in range(updates):
        if u + 1 < updates:
            scores(u + 1, stage_s[(u + 1) % 3])
        alpha = softmax(u, stage_s[u % 3], stage_p[u % 2])
        if u > 0:
            add_values(u - 1, stage_p[(u - 1) % 2])
        slot = plan(u)[0]
        acc_s[slot] = alpha * acc_s[slot]
    add_values(updates - 1, stage_p[(updates - 1) % 2])

    for slot, o_ref in ((0, oa_ref), (1, ob_ref)):
        acc = acc_s[slot]
        o_ref[...] = (acc[0:dh] / acc[dh:dh + 1]).T.astype(BF16)


def _moba_attn(qt, k, vt, sel, slopes, *, sub=4):
    s, d = k.shape
    dh = d // MOBA_HEADS
    blk = MOBA_BLOCK
    nb = s // blk
    half = nb // 2
    sub = min(sub, half)
    assert nb % (2 * sub) == 0
    scores = pltpu.VMEM((sub * blk, blk), F32)
    probs = pltpu.VMEM((sub * blk, blk), BF16)
    tile_a = lambda h, p, _: (p, h)
    return pl.pallas_call(
        functools.partial(_moba_attn_body, nb=nb, sub=sub),
        out_shape=[jax.ShapeDtypeStruct((s // 2, d), BF16)] * 2,
        grid_spec=pltpu.PrefetchScalarGridSpec(
            num_scalar_prefetch=1,
            grid=(MOBA_HEADS, half),
            in_specs=[pl.BlockSpec((dh, blk), lambda h, p, _: (h, p)),
                      pl.BlockSpec((dh, blk), lambda h, p, _: (h, nb - 1 - p)),
                      pl.BlockSpec((s, dh), lambda h, p, _: (0, h)),
                      pl.BlockSpec((dh + MOBA_ONES_ROWS, s), lambda h, p, _: (h, 0)),
                      pl.BlockSpec((1, nb, blk), lambda h, p, _: (p, h, 0)),
                      pl.BlockSpec((1, nb, blk), lambda h, p, _: (nb - 1 - p, h, 0))],
            out_specs=[pl.BlockSpec((blk, dh), tile_a),
                       pl.BlockSpec((blk, dh), lambda h, p, _: (half - 1 - p, h))],
            scratch_shapes=[pltpu.VMEM((2 * sub, blk, blk), F32),
                            pltpu.VMEM((2, dh, blk), BF16),
                            pltpu.VMEM((2, nb, blk), F32),
                            pltpu.VMEM((2, 1, blk), F32),
                            pltpu.VMEM((2, dh + MOBA_ONES_ROWS, blk), F32),
                            scores, scores, scores, probs, probs]),
        compiler_params=_params("arbitrary", "arbitrary"),
        name="moba_attn",
    )(slopes, qt, qt, k, vt, sel, sel)


def _proj_residual_body(h_ref, lo_ref, hi_ref, w_ref, o_ref, *, half_steps):
    a = jnp.where(pl.program_id(0) < half_steps, lo_ref[...], hi_ref[...])
    o_ref[...] = h_ref[...] + _dot(a, w_ref[...])


def _proj_residual(h, a_lo, a_hi, w, *, tm=512):
    s, d = h.shape
    half_steps = a_lo.shape[0] // tm
    row = pl.BlockSpec((tm, d), lambda i: (i, 0))
    return pl.pallas_call(
        functools.partial(_proj_residual_body, half_steps=half_steps),
        out_shape=jax.ShapeDtypeStruct((s, d), F32),
        grid=(s // tm,),
        in_specs=[row,
                  pl.BlockSpec((tm, a_lo.shape[1]), lambda i: (jnp.minimum(i, half_steps - 1), 0)),
                  pl.BlockSpec((tm, a_hi.shape[1]), lambda i: (jnp.maximum(i - half_steps, 0), 0)),
                  _resident(w.shape)],
        out_specs=row,
        compiler_params=_params("arbitrary"),
        name="proj_residual",
    )(h, a_lo, a_hi, w)


def _moba_mixer(h, g, wqt, wk, wvt, wo):
    s, d = h.shape
    qt, k, vt, sel = _moba_proj(h, g, wqt, wk, wvt)
    heads = jnp.arange(1, MOBA_HEADS + 1, dtype=F32)
    slopes = 2.0 ** (-8.0 * heads / MOBA_HEADS)
    o_lo, o_hi = _moba_attn(qt, k, vt, sel, slopes)
    return _proj_residual(h, o_lo, o_hi, wo)


def kernel(x, g_ffn1, g_mix, g_ffn2, g_final, w1_gate, w1_up, w1_down, w2_gate, w2_up, w2_down, cv_w_pw1, cv_b_pw1, cv_w_dw, cv_b_dw, cv_ln_g, cv_ln_b, cv_w_pw2, cv_b_pw2, gla_w_q, gla_w_k, gla_w_v, gla_w_a1, gla_w_a2, gla_b_a, gla_w_r, gla_b_r, gla_norm_g, gla_w_o, mb_w_q, mb_w_k, mb_w_v, mb_w_o):
    batch, seq, d = x.shape
    depth = g_ffn1.shape[0]
    assert batch == 1
    bf = lambda w: w.astype(BF16)
    vec = lambda b: b.reshape(1, -1)
    h = x.reshape(seq, d)
    gf = vec(g_final)
    ffn1 = [_to_bf16(w) for w in (w1_gate, w1_up, w1_down)]
    ffn2 = [_to_bf16(w) for w in (w2_gate, w2_up, w2_down)]
    cv_pw1, cv_pw2 = _to_bf16(cv_w_pw1), _to_bf16(cv_w_pw2)
    gla_q, gla_k, gla_v, gla_r, gla_o = [
        _to_bf16(w) for w in (gla_w_q, gla_w_k, gla_w_v, gla_w_r, gla_w_o)]
    mb_k, mb_o = [_to_bf16(w) for w in (mb_w_k, mb_w_o)]
    for i in range(depth):
        h = _ffn(h, vec(g_ffn1[i]), *ffn1, gf, layer=i, final_norm=False)
        kind, j = i % 3, i // 3
        gm = vec(g_mix[i])
        if kind == 0:
            h = _conv_mixer(h, gm, cv_pw1[j], vec(cv_b_pw1[j]), cv_w_dw[j], vec(cv_b_dw[j]),
                            vec(cv_ln_g[j]), vec(cv_ln_b[j]), cv_pw2[j], vec(cv_b_pw2[j]))
        elif kind == 1:
            rank = gla_w_a1.shape[2]
            wa1 = jnp.pad(gla_w_a1[j], ((0, 0), (0, GLA_RANK_PAD - rank)))
            wa2 = jnp.pad(gla_w_a2[j], ((0, GLA_RANK_PAD - rank), (0, 0)))
            h = _gla_mixer(h, gm, gla_q[j], gla_k[j], gla_v[j], gla_r[j],
                           bf(wa1), bf(wa2), vec(gla_b_a[j]), vec(gla_b_r[j]),
                           vec(gla_norm_g[j]), gla_o[j])
        else:
            h = _moba_mixer(h, gm, bf(mb_w_q[j].T), mb_k[j], bf(mb_w_v[j].T), mb_o[j])
        h = _ffn(h, vec(g_ffn2[i]), *ffn2, gf, layer=i, final_norm=(i == depth - 1))
    return h.reshape(batch, seq, d)
```
